```python
import jax, jax.numpy as jnp
from jax import lax
import numpy as np

D_MODEL = 1024
BATCH = 4
SEQ = 8192
DEPTH = 1

N_HEADS = 8
HEAD_DIM = 64
ATTN_WIDTH = N_HEADS * HEAD_DIM
CONV_CH = D_MODEL - ATTN_WIDTH
IN_COLS = 2 * CONV_CH + 3 * ATTN_WIDTH
CONV_WIDTH = 31
ROT_DIM = HEAD_DIM // 4
ROPE_THETA = 500000.0
DILATED_PATTERNS = ((128, 1), (512, 4), (2048, 16))
N_EXPERTS = 32
TOP_K = 4
D_FF = D_MODEL
SWIGLU_LIMIT = 7.0
SWIGLU_ALPHA = 1.702
MOE_BLOCK = 128
NORM_EPS = 1e-6

kernel_name = "hybrid_conv_dilated_attn_moe_adaln"

F32 = jnp.float32


def rms_norm(x, g):
    xf = x.astype(F32)
    y = xf * lax.rsqrt(jnp.mean(xf * xf, axis=-1, keepdims=True) + NORM_EPS)
    return (y * g.astype(F32)).astype(x.dtype)


def modulate(h, shift, scale):
    return h * (1.0 + scale) + shift


def rotary_tables(positions):
    inv_freq = ROPE_THETA ** (-jnp.arange(0, ROT_DIM, 2, dtype=F32) / ROT_DIM)
    ang = positions.astype(F32)[..., None] * inv_freq
    return jnp.cos(ang)[:, :, None, :], jnp.sin(ang)[:, :, None, :]


def apply_partial_rotary(t, cos, sin):
    half = ROT_DIM // 2
    tf = t.astype(F32)
    x1, x2, rest = tf[..., :half], tf[..., half:ROT_DIM], tf[..., ROT_DIM:]
    rot = jnp.concatenate([x1 * cos - x2 * sin, x2 * cos + x1 * sin, rest], axis=-1)
    return rot.astype(t.dtype)


def conformer_conv(a, g, conv_w, conv_b, ln_g, ln_b):
    u = a * jax.nn.sigmoid(g)
    y = lax.conv_general_dilated(
        u, conv_w.astype(u.dtype), window_strides=(1,),
        padding=[(CONV_WIDTH - 1, 0)],
        dimension_numbers=("NWC", "WIO", "NWC"),
        feature_group_count=CONV_CH) + conv_b
    yf = y.astype(F32)
    mu = jnp.mean(yf, axis=-1, keepdims=True)
    var = jnp.mean(jnp.square(yf - mu), axis=-1, keepdims=True)
    yn = (yf - mu) * lax.rsqrt(var + NORM_EPS) * ln_g.astype(F32) + ln_b.astype(F32)
    return jax.nn.silu(yn).astype(a.dtype)


def dilated_window_attention(q, k, v, window, dilation):
    Bb, S, H, hd = q.shape
    blk = window // dilation
    chunk = blk * dilation
    Sp = -(-S // chunk) * chunk
    L = Sp // dilation
    nb = L // blk

    def to_sub(t):
        t = jnp.pad(t, ((0, 0), (0, Sp - S), (0, 0), (0, 0)))
        t = t.reshape(Bb, L, dilation, H, hd).transpose(0, 2, 3, 1, 4)
        return t.reshape(Bb, dilation, H, nb, blk, hd)

    def with_prev(t):
        prev = jnp.pad(t, ((0, 0), (0, 0), (0, 0), (1, 0), (0, 0), (0, 0)))[:, :, :, :-1]
        return jnp.concatenate([prev, t], axis=4)

    qs = to_sub(q)
    ks = with_prev(to_sub(k))
    vs = with_prev(to_sub(v))
    s = jnp.einsum("bdhnqe,bdhnke->bdhnqk", qs.astype(F32), ks.astype(F32)) * (HEAD_DIM ** -0.5)
    qi = jnp.arange(blk)[:, None]
    kj = jnp.arange(2 * blk)[None, :]
    dist = qi + blk - kj
    band = (dist >= 0) & (dist <= blk)
    valid = band[None] & ((jnp.arange(nb)[:, None, None] > 0) | (kj >= blk)[None])
    s = jnp.where(valid, s, -jnp.inf)
    m = jnp.max(s, axis=-1)
    p = jnp.exp(s - m[..., None])
    l = jnp.sum(p, axis=-1)
    o = jnp.einsum("bdhnqk,bdhnke->bdhnqe", p, vs.astype(F32)) / l[..., None]

    o = o.reshape(Bb, dilation, H, L, hd).transpose(0, 3, 1, 2, 4).reshape(Bb, Sp, H, hd)[:, :S]
    m = m.reshape(Bb, dilation, H, L).transpose(0, 3, 1, 2).reshape(Bb, Sp, H)[:, :S]
    l = l.reshape(Bb, dilation, H, L).transpose(0, 3, 1, 2).reshape(Bb, Sp, H)[:, :S]
    return o, m, l


def dilated_mixture_attention(q, k, v):
    res = [dilated_window_attention(q, k, v, w, d) for (w, d) in DILATED_PATTERNS]
    m_all = jnp.max(jnp.stack([r[1] for r in res], axis=0), axis=0)
    wts = [jnp.exp(r[1] - m_all) * r[2] for r in res]
    num = sum(w[..., None] * r[0] for w, r in zip(wts, res))
    den = sum(wts)
    return num / den[..., None]


def routed_ffn(h, w_router, b_router, w_gu, b_gu, w_down, b_down):
    Bb, S, D = h.shape
    T = Bb * S
    hf = h.reshape(T, D)
    logits = hf.astype(F32) @ w_router.astype(F32) + b_router.astype(F32)
    top_val, top_idx = lax.top_k(logits, TOP_K)
    gates = jax.nn.softmax(top_val, axis=-1)

    A = T * TOP_K
    e_flat = top_idx.reshape(A).astype(jnp.int32)
    tok_flat = jnp.repeat(jnp.arange(T, dtype=jnp.int32), TOP_K)
    g_flat = gates.reshape(A)
    e_sorted, order = lax.sort((e_flat, jnp.arange(A, dtype=jnp.int32)), num_keys=1, is_stable=True)
    counts = jnp.bincount(e_flat, length=N_EXPERTS).astype(jnp.int32)
    starts = jnp.cumsum(counts) - counts
    padded = (counts + MOE_BLOCK - 1) // MOE_BLOCK * MOE_BLOCK
    pends = jnp.cumsum(padded)
    pstarts = pends - padded
    dest = pstarts[e_sorted] + (jnp.arange(A, dtype=jnp.int32) - starts[e_sorted])
    n_blocks = -(-A // MOE_BLOCK) + N_EXPERTS
    rows = n_blocks * MOE_BLOCK
    tok_pad = jnp.full((rows,), T, jnp.int32).at[dest].set(tok_flat[order])
    gate_pad = jnp.zeros((rows,), h.dtype).at[dest].set(g_flat[order].astype(h.dtype))
    block_start = jnp.arange(n_blocks, dtype=jnp.int32) * MOE_BLOCK
    block_expert = jnp.minimum(jnp.searchsorted(pends, block_start, side="right"), N_EXPERTS - 1).astype(jnp.int32)
    h_ext = jnp.concatenate([hf, jnp.zeros((1, D), hf.dtype)], axis=0)

    def expert_block(args):
        tok, e = args
        xb = h_ext[tok]
        gu = xb @ w_gu[e] + b_gu[e]
        gate = jnp.minimum(gu[:, :D_FF], SWIGLU_LIMIT)
        up = jnp.clip(gu[:, D_FF:], -SWIGLU_LIMIT, SWIGLU_LIMIT)
        act = (up + 1.0) * (gate * jax.nn.sigmoid(SWIGLU_ALPHA * gate))
        return act @ w_down[e] + b_down[e]

    y = lax.map(expert_block, (tok_pad.reshape(n_blocks, MOE_BLOCK), block_expert))
    y = y.reshape(rows, D) * gate_pad[:, None]
    out = jnp.zeros((T + 1, D), y.dtype).at[tok_pad].add(y)[:T]
    return out.reshape(Bb, S, D).astype(h.dtype)


def setup_inputs(seed: int = 0) -> dict:
    key = jax.random.key(seed)
    ks = jax.random.split(key, 24)

    def nrm(k, shape, scale):
        return jax.random.normal(k, shape, F32) * scale

    x = nrm(ks[0], (BATCH, SEQ, D_MODEL), 1.0)
    c = nrm(ks[1], (BATCH, D_MODEL), 1.0)
    positions = (jax.random.randint(ks[2], (BATCH, 1), 0, 4096, dtype=jnp.int32)
                 + jnp.arange(SEQ, dtype=jnp.int32)[None, :])
    return {
        "x": x,
        "c": c,
        "positions": positions,
        "w_ada": nrm(ks[3], (DEPTH, D_MODEL, 6 * D_MODEL), 0.5 * D_MODEL ** -0.5),
        "b_ada": nrm(ks[4], (DEPTH, 6 * D_MODEL), 0.02),
        "g_mix": 1.0 + nrm(ks[5], (DEPTH, D_MODEL), 0.05),
        "w_in": nrm(ks[6], (DEPTH, D_MODEL, IN_COLS), D_MODEL ** -0.5),
        "conv_w": nrm(ks[7], (DEPTH, CONV_WIDTH, 1, CONV_CH), CONV_WIDTH ** -0.5),
        "conv_b": nrm(ks[8], (DEPTH, CONV_CH), 0.02),
        "conv_ln_g": 1.0 + nrm(ks[9], (DEPTH, CONV_CH), 0.05),
        "conv_ln_b": nrm(ks[10], (DEPTH, CONV_CH), 0.02),
        "w_out": nrm(ks[11], (DEPTH, D_MODEL, D_MODEL), D_MODEL ** -0.5),
        "g_ffn": 1.0 + nrm(ks[12], (DEPTH, D_MODEL), 0.05),
        "w_router": nrm(ks[13], (DEPTH, D_MODEL, N_EXPERTS), D_MODEL ** -0.5),
        "b_router": nrm(ks[14], (DEPTH, N_EXPERTS), 0.01),
        "w_gu": nrm(ks[15], (DEPTH, N_EXPERTS, D_MODEL, 2 * D_FF), D_MODEL ** -0.5),
        "b_gu": nrm(ks[16], (DEPTH, N_EXPERTS, 2 * D_FF), 0.02),
        "w_down": nrm(ks[17], (DEPTH, N_EXPERTS, D_FF, D_MODEL), D_FF ** -0.5),
        "b_down": nrm(ks[18], (DEPTH, N_EXPERTS, D_MODEL), 0.02),
        "g_final": 1.0 + nrm(ks[19], (D_MODEL,), 0.05),
    }


def reference(x, c, positions, w_ada, b_ada, g_mix, w_in, conv_w, conv_b, conv_ln_g, conv_ln_b,
              w_out, g_ffn, w_router, b_router, w_gu, b_gu, w_down, b_down, g_final):
    Bb, S, _ = x.shape
    cos, sin = rotary_tables(positions)
    c_act = jax.nn.silu(c)
    splits = [CONV_CH, 2 * CONV_CH, 2 * CONV_CH + ATTN_WIDTH, 2 * CONV_CH + 2 * ATTN_WIDTH]
    for l in range(DEPTH):
        mod = (c_act @ w_ada[l] + b_ada[l])[:, None, :]
        sh1, sc1, gt1, sh2, sc2, gt2 = jnp.split(mod, 6, axis=-1)

        h = modulate(rms_norm(x, g_mix[l]), sh1, sc1)
        proj = h @ w_in[l]
        a, g, q, k, v = jnp.split(proj, splits, axis=-1)
        conv_out = conformer_conv(a, g, conv_w[l], conv_b[l], conv_ln_g[l], conv_ln_b[l])
        q = apply_partial_rotary(q.reshape(Bb, S, N_HEADS, HEAD_DIM), cos, sin)
        k = apply_partial_rotary(k.reshape(Bb, S, N_HEADS, HEAD_DIM), cos, sin)
        v = v.reshape(Bb, S, N_HEADS, HEAD_DIM)
        attn_out = dilated_mixture_attention(q, k, v).reshape(Bb, S, ATTN_WIDTH).astype(x.dtype)
        mixed = jnp.concatenate([conv_out, attn_out], axis=-1) @ w_out[l]
        x = x + gt1 * mixed

        h2 = modulate(rms_norm(x, g_ffn[l]), sh2, sc2)
        x = x + gt2 * routed_ffn(h2, w_router[l], b_router[l], w_gu[l], b_gu[l], w_down[l], b_down[l])
    return rms_norm(x, g_final)
```

```python
import functools

import jax
import jax.numpy as jnp
from jax import lax
from jax.experimental import pallas as pl
from jax.experimental.pallas import tpu as pltpu

F32 = jnp.float32
BF16 = jnp.bfloat16
HIGHEST = lax.Precision.HIGHEST

D_MODEL = 1024
N_HEADS = 8
HEAD_DIM = 64
ATTN_WIDTH = N_HEADS * HEAD_DIM
CONV_CH = D_MODEL - ATTN_WIDTH
IN_COLS = 2 * CONV_CH + 3 * ATTN_WIDTH
CONV_WIDTH = 31
ROT_DIM = HEAD_DIM // 4
ROPE_THETA = 500000.0
DILATIONS = (1, 4, 16)
ATTN_BLK = 128
N_EXPERTS = 32
TOP_K = 4
D_FF = D_MODEL
SWIGLU_LIMIT = 7.0
SWIGLU_ALPHA = 1.702
NORM_EPS = 1e-6

LANES = 128
SUBLANES = 8
SLABS = D_MODEL // LANES
VMEM_LIMIT = 56 * 1024 * 1024

PROJ_ROWS = 512
CONV_HALO = 32
CONV_ROWS = 64
ATTN_CHUNK = 2048
MOE_ROWS = 512
MOE_COLS = 256
TOK_TILE = 256


def _ada_kernel(c_ref, w_ref, b_ref, o_ref):
    c = c_ref[...]
    c_act = c * jax.nn.sigmoid(c)
    o_ref[...] = jnp.dot(c_act, w_ref[...], preferred_element_type=F32, precision=HIGHEST) + b_ref[...]


def _rms_modulate(x, gain, shift, scale):
    ms = jnp.mean(x * x, axis=-1, keepdims=True)
    y = x * lax.rsqrt(ms + NORM_EPS) * gain
    return y * (1.0 + scale) + shift


def _proj_kernel(x_ref, mod_ref, gmix_ref, win_ref, cw_ref, cb_ref, lng_ref, lnb_ref,
                 rc_ref, rsa_ref, rsb_ref, co_ref, q_ref, k_ref, v_ref, ubuf):
    rows = x_ref.shape[1]
    h = _rms_modulate(x_ref[0], gmix_ref[...], mod_ref[0, 0:1, :], mod_ref[0, 1:2, :])
    proj = jnp.dot(h.astype(BF16), win_ref[...], preferred_element_type=F32)

    u = proj[:, :CONV_CH] * jax.nn.sigmoid(proj[:, CONV_CH:2 * CONV_CH])

    @pl.when(pl.program_id(1) == 0)
    def _():
        ubuf[0:CONV_HALO, :] = jnp.zeros((CONV_HALO, CONV_CH), F32)

    ubuf[CONV_HALO:, :] = u
    first_tap = CONV_HALO - (CONV_WIDTH - 1)
    for ch in range(rows // CONV_ROWS):
        base = ch * CONV_ROWS + first_tap
        acc = jnp.broadcast_to(cb_ref[...], (CONV_ROWS, CONV_CH))
        for j in range(CONV_WIDTH):
            acc = acc + cw_ref[j:j + 1, :] * ubuf[base + j:base + j + CONV_ROWS, :]
        mu = jnp.mean(acc, axis=-1, keepdims=True)
        cen = acc - mu
        var = jnp.mean(cen * cen, axis=-1, keepdims=True)
        yn = cen * lax.rsqrt(var + NORM_EPS) * lng_ref[...] + lnb_ref[...]
        co_ref[0, ch * CONV_ROWS:(ch + 1) * CONV_ROWS, :] = (yn * jax.nn.sigmoid(yn)).astype(co_ref.dtype)
    ubuf[0:CONV_HALO, :] = ubuf[rows:rows + CONV_HALO, :]

    rc, rsa, rsb = rc_ref[0], rsa_ref[0], rsb_ref[0]
    half = ROT_DIM // 2
    q0 = 2 * CONV_CH
    k0 = q0 + ATTN_WIDTH
    v0 = k0 + ATTN_WIDTH
    for blk in range(ATTN_WIDTH // LANES):
        lo, hi = blk * LANES, (blk + 1) * LANES
        for src0, dst_ref, scale in ((q0, q_ref, HEAD_DIM ** -0.5), (k0, k_ref, 1.0)):
            t = proj[:, src0 + lo:src0 + hi]
            rot = t * rc + pltpu.roll(t, LANES - half, 1) * rsa + pltpu.roll(t, half, 1) * rsb
            dst_ref[0, :, lo:hi] = rot * scale
    v_ref[0] = proj[:, v0:v0 + ATTN_WIDTH]


def _attn_kernel(q_ref, kp_ref, kc_ref, vp_ref, vc_ref, o_ref, kk, vv, m_s, l_s, acc_s):
    chunk = pl.program_id(2)
    kk[0:ATTN_CHUNK, :] = kp_ref[0]
    kk[ATTN_CHUNK:, :] = kc_ref[0]
    vv[0:ATTN_CHUNK, :] = vp_ref[0]
    vv[ATTN_CHUNK:, :] = vc_ref[0]

    head0 = lax.broadcasted_iota(jnp.int32, (ATTN_BLK, LANES), 1) < HEAD_DIM
    qi = lax.broadcasted_iota(jnp.int32, (ATTN_BLK, 2 * ATTN_BLK), 0)
    kj = lax.broadcasted_iota(jnp.int32, (ATTN_BLK, 2 * ATTN_BLK), 1)
    dist = qi + ATTN_BLK - kj
    band = jnp.logical_and(dist >= 0, dist <= ATTN_BLK)
    neg_inf = jnp.float32(-jnp.inf)
    bias_band = jnp.where(band, 0.0, neg_inf)
    bias_first = jnp.where(jnp.logical_and(band, kj >= ATTN_BLK), 0.0, neg_inf)

    def pair(col0, col1):
        return jnp.where(head0, col0, col1)

    for pi, d in enumerate(DILATIONS):
        span = ATTN_BLK * d

        def body(idx, carry, d=d, span=span, first_pattern=(pi == 0)):
            u = idx // d
            r = idx % d
            q_start = u * span + r
            k_start = ATTN_CHUNK + q_start - span
            qb = q_ref[0, pl.ds(q_start, ATTN_BLK, stride=d), :]
            kb = kk[pl.ds(k_start, 2 * ATTN_BLK, stride=d), :].astype(BF16)
            vb = vv[pl.ds(k_start, 2 * ATTN_BLK, stride=d), :].astype(BF16)
            no_prev = jnp.logical_and(chunk == 0, u == 0)
            bias = jnp.where(no_prev, bias_first, bias_band)

            scores, mb = [], []
            for hsel in (True, False):
                qh = jnp.where(head0 == hsel, qb, 0.0).astype(BF16)
                s = lax.dot_general(qh, kb, (((1,), (1,)), ((), ())), preferred_element_type=F32) + bias
                scores.append(s)
                mb.append(jnp.max(s, axis=1, keepdims=True))

            rows = pl.ds(q_start, ATTN_BLK, stride=d)
            if first_pattern:
                mcol = mb
                m_new = pair(*mb)
            else:
                m_old = m_s[rows, :]
                mcol = [jnp.maximum(m_old[:, 0:1], mb[0]), jnp.maximum(m_old[:, HEAD_DIM:HEAD_DIM + 1], mb[1])]
                m_new = pair(*mcol)
                alpha = jnp.exp(m_old - m_new)

            lb, pv = [], []
            for hidx in range(2):
                p = jnp.exp(scores[hidx] - mcol[hidx])
                lb.append(jnp.sum(p, axis=1, keepdims=True))
                pv.append(jnp.dot(p.astype(BF16), vb, preferred_element_type=F32))
            l_new = pair(*lb)
            acc_new = jnp.where(head0, pv[0], pv[1])
            if not first_pattern:
                l_new = l_new + alpha * l_s[rows, :]
                acc_new = acc_new + alpha * acc_s[rows, :]
            m_s[rows, :] = m_new
            l_s[rows, :] = l_new
            acc_s[rows, :] = acc_new
            return carry

        lax.fori_loop(0, ATTN_CHUNK // ATTN_BLK, body, 0)

    o_ref[0] = (acc_s[...] / l_s[...]).astype(o_ref.dtype)


def _mix_kernel(co_ref, ao_ref, x_ref, mod_ref, wo_ref, gffn_ref, wr_ref, br_ref, x1_ref, h2_ref, lg_ref):
    rows = x_ref.shape[1]
    mixed = (jnp.dot(co_ref[0], wo_ref[0:CONV_CH, :], preferred_element_type=F32)
             + jnp.dot(ao_ref[0], wo_ref[CONV_CH:, :], preferred_element_type=F32))
    x1 = x_ref[0] + mod_ref[0, 2:3, :] * mixed
    x1_ref[0] = x1
    h2 = _rms_modulate(x1, gffn_ref[...], mod_ref[0, 3:4, :], mod_ref[0, 4:5, :])
    for j in range(SLABS):
        h2_ref[pl.ds(j, rows, stride=SLABS), :] = h2[:, j * LANES:(j + 1) * LANES]
    lg_ref[...] = jnp.dot(h2, wr_ref[...], preferred_element_type=F32, precision=HIGHEST) + br_ref[...]


def _slab_rows(idx):
    return pl.ds(pl.multiple_of(idx * SLABS, SLABS), SLABS)


def _dispatch_kernel(dest_ref, h2_ref, xs_ref, sem):
    base = pl.program_id(0) * (TOK_TILE * TOP_K)

    def copy(a):
        return pltpu.make_async_copy(h2_ref.at[_slab_rows(a // TOP_K), :],
                                     xs_ref.at[_slab_rows(dest_ref[base + a]), :], sem)

    def start(a, carry):
        copy(a).start()
        return carry

    def wait(a, carry):
        copy(a).wait()
        return carry

    lax.fori_loop(0, TOK_TILE * TOP_K, start, 0, unroll=8)
    lax.fori_loop(0, TOK_TILE * TOP_K, wait, 0, unroll=8)


def _expert_kernel(meta_ref, xs_ref, wgu_ref, bgu_ref, wd_ref, bd_ref, ys_ref, wgu_bf, wd_bf):
    i = pl.program_id(0)
    n_used = meta_ref[0]
    expert = meta_ref[1 + i]
    prev_expert = meta_ref[jnp.maximum(i, 1)]

    @pl.when(jnp.logical_or(i == 0, expert != prev_expert))
    def _():
        wgu_bf[...] = wgu_ref[0].astype(BF16)
        wd_bf[...] = wd_ref[0].astype(BF16)

    @pl.when(i < n_used)
    def _():
        x = jnp.concatenate([xs_ref[pl.ds(j, MOE_ROWS, stride=SLABS), :] for j in range(SLABS)],
                            axis=1).astype(BF16)
        y = jnp.broadcast_to(bd_ref[0], (MOE_ROWS, D_MODEL))
        for cidx in range(D_FF // MOE_COLS):
            lo = cidx * MOE_COLS
            gate = jnp.dot(x, wgu_bf[:, lo:lo + MOE_COLS], preferred_element_type=F32) + bgu_ref[0, :, lo:lo + MOE_COLS]
            up = (jnp.dot(x, wgu_bf[:, D_FF + lo:D_FF + lo + MOE_COLS], preferred_element_type=F32)
                  + bgu_ref[0, :, D_FF + lo:D_FF + lo + MOE_COLS])
            gate = jnp.minimum(gate, SWIGLU_LIMIT)
            up = jnp.clip(up, -SWIGLU_LIMIT, SWIGLU_LIMIT)
            act = (up + 1.0) * (gate * jax.nn.sigmoid(SWIGLU_ALPHA * gate))
            y = y + jnp.dot(act.astype(BF16), wd_bf[lo:lo + MOE_COLS, :], preferred_element_type=F32)
        for j in range(SLABS):
            ys_ref[pl.ds(j, MOE_ROWS, stride=SLABS), :] = y[:, j * LANES:(j + 1) * LANES]


def _combine_kernel(dest_ref, x1_ref, gate_ref, mod_ref, gfin_ref, ys_ref, o_ref, buf, sems):
    i = pl.program_id(0)
    n_tiles = pl.num_programs(0)
    per_tile = TOK_TILE * TOP_K

    def copy(tile, slot, a):
        return pltpu.make_async_copy(ys_ref.at[_slab_rows(dest_ref[tile * per_tile + a]), :],
                                     buf.at[slot, _slab_rows(a), :], sems.at[slot])

    def start_tile(tile, slot):
        def start(a, carry):
            copy(tile, slot, a).start()
            return carry
        lax.fori_loop(0, per_tile, start, 0, unroll=8)

    @pl.when(i == 0)
    def _():
        start_tile(0, 0)

    @pl.when(i + 1 < n_tiles)
    def _():
        start_tile(i + 1, (i + 1) % 2)

    slot = i % 2

    def wait(a, carry):
        copy(i, slot, a).wait()
        return carry
    lax.fori_loop(0, per_tile, wait, 0, unroll=8)

    gates = gate_ref[...]
    stride = TOP_K * SLABS
    cols = []
    for j in range(SLABS):
        acc = gates[:, 0:1] * buf[slot, pl.ds(j, TOK_TILE, stride=stride), :]
        for k in range(1, TOP_K):
            acc = acc + gates[:, k:k + 1] * buf[slot, pl.ds(k * SLABS + j, TOK_TILE, stride=stride), :]
        cols.append(acc)
    ffn = jnp.concatenate(cols, axis=1)
    x2 = x1_ref[...] + mod_ref[0, 5:6, :] * ffn
    ms = jnp.mean(x2 * x2, axis=-1, keepdims=True)
    o_ref[...] = x2 * lax.rsqrt(ms + NORM_EPS) * gfin_ref[...]


def _rotary_lane_tables(positions):
    half = ROT_DIM // 2
    inv_freq = ROPE_THETA ** (-jnp.arange(0, ROT_DIM, 2, dtype=F32) / ROT_DIM)
    ang = positions.astype(F32)[..., None] * inv_freq
    cos, sin = jnp.cos(ang), jnp.sin(ang)
    ones = jnp.ones(cos.shape[:-1] + (HEAD_DIM - ROT_DIM,), F32)
    zeros = jnp.zeros_like(ones)
    zero_h = jnp.zeros_like(sin)
    rc = jnp.concatenate([cos, cos, ones], axis=-1)
    rsa = jnp.concatenate([-sin, zero_h, zeros], axis=-1)
    rsb = jnp.concatenate([zero_h, sin, zeros], axis=-1)
    reps = LANES // HEAD_DIM
    return tuple(jnp.tile(t, (1, 1, reps)) for t in (rc, rsa, rsb))


def _routing(logits, n_rows_block):
    n_tok = logits.shape[0]
    top_val, top_idx = lax.top_k(logits, TOP_K)
    gates = jax.nn.softmax(top_val, axis=-1)
    onehot = jnp.sum((top_idx[:, :, None] == jnp.arange(N_EXPERTS, dtype=jnp.int32)).astype(jnp.int32), axis=1)
    csum = jnp.cumsum(onehot, axis=0)
    rank = csum - onehot
    counts = csum[-1]
    padded = (counts + n_rows_block - 1) // n_rows_block * n_rows_block
    pends = jnp.cumsum(padded)
    pstarts = pends - padded
    dest = jnp.take_along_axis(rank + pstarts[None, :], top_idx, axis=1).astype(jnp.int32)
    n_blocks = n_tok * TOP_K // n_rows_block + N_EXPERTS
    n_used = (pends[-1] // n_rows_block).astype(jnp.int32)
    block_start = jnp.arange(n_blocks, dtype=jnp.int32) * n_rows_block
    block_start = jnp.minimum(block_start, pends[-1] - n_rows_block)
    block_expert = jnp.minimum(jnp.searchsorted(pends, block_start, side="right"), N_EXPERTS - 1).astype(jnp.int32)
    meta = jnp.concatenate([n_used[None], block_expert])
    return gates, dest.reshape(-1), meta, n_blocks


def kernel(x, c, positions, w_ada, b_ada, g_mix, w_in, conv_w, conv_b, conv_ln_g, conv_ln_b,
           w_out, g_ffn, w_router, b_router, w_gu, b_gu, w_down, b_down, g_final):
    bsz, seq, d = x.shape
    n_tok = bsz * seq
    depth = w_ada.shape[0]
    assert depth == 1, "the final RMSNorm is fused into the layer's combine kernel"
    rc, rsa, rsb = _rotary_lane_tables(positions)
    s_tiles = seq // PROJ_ROWS
    arb2 = pltpu.CompilerParams(dimension_semantics=("arbitrary", "arbitrary"), vmem_limit_bytes=VMEM_LIMIT)
    arb1 = pltpu.CompilerParams(dimension_semantics=("arbitrary",), vmem_limit_bytes=VMEM_LIMIT)

    for l in range(depth):
        n_mod = 6 * d
        mod = pl.pallas_call(
            _ada_kernel,
            grid=(n_mod // d,),
            in_specs=[pl.BlockSpec((bsz, d), lambda j: (0, 0)),
                      pl.BlockSpec((d, d), lambda j: (0, j)),
                      pl.BlockSpec((1, d), lambda j: (0, j))],
            out_specs=pl.BlockSpec((bsz, d), lambda j: (0, j)),
            out_shape=jax.ShapeDtypeStruct((bsz, n_mod), F32),
            name="ada",
        )(c, w_ada[l], b_ada[l].reshape(1, n_mod))
        mod = mod.reshape(bsz, 6, d)

        row_spec = lambda width: pl.BlockSpec((1, PROJ_ROWS, width), lambda b, s: (b, s, 0))
        full2 = lambda shape: pl.BlockSpec(shape, lambda b, s: (0, 0))
        mod_spec = pl.BlockSpec((1, 6, d), lambda b, s: (b, 0, 0))
        conv_out, q, k, v = pl.pallas_call(
            _proj_kernel,
            grid=(bsz, s_tiles),
            in_specs=[row_spec(d), mod_spec, full2((1, d)), full2((d, IN_COLS)),
                      full2((CONV_WIDTH, CONV_CH)), full2((1, CONV_CH)), full2((1, CONV_CH)), full2((1, CONV_CH)),
                      row_spec(LANES), row_spec(LANES), row_spec(LANES)],
            out_specs=[row_spec(CONV_CH), row_spec(ATTN_WIDTH), row_spec(ATTN_WIDTH), row_spec(ATTN_WIDTH)],
            out_shape=[jax.ShapeDtypeStruct((bsz, seq, CONV_CH), BF16),
                       jax.ShapeDtypeStruct((bsz, seq, ATTN_WIDTH), F32),
                       jax.ShapeDtypeStruct((bsz, seq, ATTN_WIDTH), F32),
                       jax.ShapeDtypeStruct((bsz, seq, ATTN_WIDTH), F32)],
            scratch_shapes=[pltpu.VMEM((PROJ_ROWS + CONV_HALO, CONV_CH), F32)],
            compiler_params=arb2,
            name="proj",
        )(x, mod, g_mix[l].reshape(1, d), w_in[l].astype(BF16), conv_w[l].reshape(CONV_WIDTH, CONV_CH),
          conv_b[l].reshape(1, CONV_CH), conv_ln_g[l].reshape(1, CONV_CH), conv_ln_b[l].reshape(1, CONV_CH),
          rc, rsa, rsb)

        cur = pl.BlockSpec((1, ATTN_CHUNK, LANES), lambda b, p, ch: (b, ch, p))
        prev = pl.BlockSpec((1, ATTN_CHUNK, LANES), lambda b, p, ch: (b, jnp.maximum(ch - 1, 0), p))
        attn_out = pl.pallas_call(
            _attn_kernel,
            grid=(bsz, ATTN_WIDTH // LANES, seq // ATTN_CHUNK),
            in_specs=[cur, prev, cur, prev, cur],
            out_specs=cur,
            out_shape=jax.ShapeDtypeStruct((bsz, seq, ATTN_WIDTH), BF16),
            scratch_shapes=[pltpu.VMEM((2 * ATTN_CHUNK, LANES), F32), pltpu.VMEM((2 * ATTN_CHUNK, LANES), F32),
                            pltpu.VMEM((ATTN_CHUNK, LANES), F32), pltpu.VMEM((ATTN_CHUNK, LANES), F32),
                            pltpu.VMEM((ATTN_CHUNK, LANES), F32)],
            compiler_params=pltpu.CompilerParams(dimension_semantics=("arbitrary",) * 3,
                                                 vmem_limit_bytes=VMEM_LIMIT),
            name="attn",
        )(q, k, k, v, v)

        x1, h2, logits = pl.pallas_call(
            _mix_kernel,
            grid=(bsz, s_tiles),
            in_specs=[row_spec(CONV_CH), row_spec(ATTN_WIDTH), row_spec(d), mod_spec, full2((d, d)),
                      full2((1, d)), full2((d, N_EXPERTS)), full2((1, N_EXPERTS))],
            out_specs=[row_spec(d),
                       pl.BlockSpec((PROJ_ROWS * SLABS, LANES), lambda b, s: (b * s_tiles + s, 0)),
                       pl.BlockSpec((PROJ_ROWS, N_EXPERTS), lambda b, s: (b * s_tiles + s, 0))],
            out_shape=[jax.ShapeDtypeStruct((bsz, seq, d), F32),
                       jax.ShapeDtypeStruct((n_tok * SLABS, LANES), F32),
                       jax.ShapeDtypeStruct((n_tok, N_EXPERTS), F32)],
            compiler_params=arb2,
            name="mix",
        )(conv_out, attn_out, x, mod, w_out[l].astype(BF16), g_ffn[l].reshape(1, d),
          w_router[l], b_router[l].reshape(1, N_EXPERTS))

        gates, dest, meta, n_blocks = _routing(logits, MOE_ROWS)
        n_rows = n_blocks * MOE_ROWS
        n_tiles = n_tok // TOK_TILE

        xs = pl.pallas_call(
            _dispatch_kernel,
            grid_spec=pltpu.PrefetchScalarGridSpec(
                num_scalar_prefetch=1,
                grid=(n_tiles,),
                in_specs=[pl.BlockSpec((TOK_TILE * SLABS, LANES), lambda i, dest: (i, 0))],
                out_specs=pl.BlockSpec(memory_space=pl.ANY),
                scratch_shapes=[pltpu.SemaphoreType.DMA(())]),
            out_shape=jax.ShapeDtypeStruct((n_rows * SLABS, LANES), F32),
            compiler_params=arb1,
            name="dispatch",
        )(dest, h2)

        def row_block(i, meta):
            return (jnp.minimum(i, meta[0] - 1), 0)

        def expert_block(i, meta):
            return (meta[1 + i], 0, 0)

        ys = pl.pallas_call(
            _expert_kernel,
            grid_spec=pltpu.PrefetchScalarGridSpec(
                num_scalar_prefetch=1,
                grid=(n_blocks,),
                in_specs=[pl.BlockSpec((MOE_ROWS * SLABS, LANES), row_block),
                          pl.BlockSpec((1, d, 2 * D_FF), expert_block),
                          pl.BlockSpec((1, 1, 2 * D_FF), expert_block),
                          pl.BlockSpec((1, D_FF, d), expert_block),
                          pl.BlockSpec((1, 1, d), expert_block)],
                out_specs=pl.BlockSpec((MOE_ROWS * SLABS, LANES), row_block),
                scratch_shapes=[pltpu.VMEM((d, 2 * D_FF), BF16), pltpu.VMEM((D_FF, d), BF16)]),
            out_shape=jax.ShapeDtypeStruct((n_rows * SLABS, LANES), F32),
            compiler_params=arb1,
            name="experts",
        )(meta, xs, w_gu[l], b_gu[l].reshape(N_EXPERTS, 1, 2 * D_FF), w_down[l], b_down[l].reshape(N_EXPERTS, 1, d))

        out = pl.pallas_call(
            _combine_kernel,
            grid_spec=pltpu.PrefetchScalarGridSpec(
                num_scalar_prefetch=1,
                grid=(n_tiles,),
                in_specs=[pl.BlockSpec((TOK_TILE, d), lambda i, dest: (i, 0)),
                          pl.BlockSpec((TOK_TILE, TOP_K), lambda i, dest: (i, 0)),
                          pl.BlockSpec((1, 6, d), lambda i, dest: (i // (seq // TOK_TILE), 0, 0)),
                          pl.BlockSpec((1, d), lambda i, dest: (0, 0)),
                          pl.BlockSpec(memory_space=pl.ANY)],
                out_specs=pl.BlockSpec((TOK_TILE, d), lambda i, dest: (i, 0)),
                scratch_shapes=[pltpu.VMEM((2, TOK_TILE * TOP_K * SLABS, LANES), F32),
                                pltpu.SemaphoreType.DMA((2,))]),
            out_shape=jax.ShapeDtypeStruct((n_tok, d), F32),
            compiler_params=arb1,
            name="combine",
        )(dest, x1.reshape(n_tok, d), gates, mod, g_final.reshape(1, d), ys)
        x = out.reshape(bsz, seq, d)
    return x
```

```python
import functools

import jax
import jax.numpy as jnp
from jax import lax
from jax.experimental import pallas as pl
from jax.experimental.pallas import tpu as pltpu

F32 = jnp.float32
BF16 = jnp.bfloat16
HIGHEST = lax.Precision.HIGHEST

D_MODEL = 1024
N_HEADS = 8
HEAD_DIM = 64
ATTN_WIDTH = N_HEADS * HEAD_DIM
CONV_CH = D_MODEL - ATTN_WIDTH
IN_COLS = 2 * CONV_CH + 3 * ATTN_WIDTH
CONV_WIDTH = 31
ROT_DIM = HEAD_DIM // 4
ROPE_THETA = 500000.0
DILATIONS = (1, 4, 16)
ATTN_BLK = 128
N_EXPERTS = 32
TOP_K = 4
D_FF = D_MODEL
SWIGLU_LIMIT = 7.0
SWIGLU_ALPHA = 1.702
NORM_EPS = 1e-6

LANES = 128
SUBLANES = 8
SLABS = D_MODEL // LANES
VMEM_LIMIT = 56 * 1024 * 1024

PROJ_ROWS = 512
CONV_HALO = 32
CONV_ROWS = 64
ATTN_CHUNK = 2048
MOE_ROWS = 512
MOE_COLS = 256
TOK_TILE = 256


def _ada_kernel(c_ref, w_ref, b_ref, o_ref):
    c = c_ref[...]
    c_act = c * jax.nn.sigmoid(c)
    o_ref[...] = jnp.dot(c_act, w_ref[...], preferred_element_type=F32, precision=HIGHEST) + b_ref[...]


def _rms_modulate(x, gain, shift, scale):
    ms = jnp.mean(x * x, axis=-1, keepdims=True)
    y = x * lax.rsqrt(ms + NORM_EPS) * gain
    return y * (1.0 + scale) + shift


def _proj_kernel(x_ref, mod_ref, gmix_ref, win_ref, cw_ref, cb_ref, lng_ref, lnb_ref,
                 rc_ref, rsa_ref, rsb_ref, co_ref, q_ref, k_ref, v_ref, ubuf):
    rows = x_ref.shape[1]
    h = _rms_modulate(x_ref[0], gmix_ref[...], mod_ref[0, 0:1, :], mod_ref[0, 1:2, :])
    proj = jnp.dot(h.astype(BF16), win_ref[...], preferred_element_type=F32)

    u = proj[:, :CONV_CH] * jax.nn.sigmoid(proj[:, CONV_CH:2 * CONV_CH])

    @pl.when(pl.program_id(1) == 0)
    def _():
        ubuf[0:CONV_HALO, :] = jnp.zeros((CONV_HALO, CONV_CH), F32)

    ubuf[CONV_HALO:, :] = u
    first_tap = CONV_HALO - (CONV_WIDTH - 1)
    for ch in range(rows // CONV_ROWS):
        base = ch * CONV_ROWS + first_tap
        acc = jnp.broadcast_to(cb_ref[...], (CONV_ROWS, CONV_CH))
        for j in range(CONV_WIDTH):
            acc = acc + cw_ref[j:j + 1, :] * ubuf[base + j:base + j + CONV_ROWS, :]
        mu = jnp.mean(acc, axis=-1, keepdims=True)
        cen = acc - mu
        var = jnp.mean(cen * cen, axis=-1, keepdims=True)
        yn = cen * lax.rsqrt(var + NORM_EPS) * lng_ref[...] + lnb_ref[...]
        co_ref[0, ch * CONV_ROWS:(ch + 1) * CONV_ROWS, :] = (yn * jax.nn.sigmoid(yn)).astype(co_ref.dtype)
    ubuf[0:CONV_HALO, :] = ubuf[rows:rows + CONV_HALO, :]

    rc, rsa, rsb = rc_ref[0], rsa_ref[0], rsb_ref[0]
    half = ROT_DIM // 2
    q0 = 2 * CONV_CH
    k0 = q0 + ATTN_WIDTH
    v0 = k0 + ATTN_WIDTH
    for blk in range(ATTN_WIDTH // LANES):
        lo, hi = blk * LANES, (blk + 1) * LANES
        for src0, dst_ref, scale in ((q0, q_ref, HEAD_DIM ** -0.5), (k0, k_ref, 1.0)):
            t = proj[:, src0 + lo:src0 + hi]
            rot = t * rc + pltpu.roll(t, LANES - half, 1) * rsa + pltpu.roll(t, half, 1) * rsb
            dst_ref[0, :, lo:hi] = rot * scale
    v_ref[0] = proj[:, v0:v0 + ATTN_WIDTH]


def _attn_kernel(q_ref, kp_ref, kc_ref, vp_ref, vc_ref, o_ref, kk, vv, m_s, l_s, acc_s):
    chunk = pl.program_id(2)
    kk[0:ATTN_CHUNK, :] = kp_ref[0]
    kk[ATTN_CHUNK:, :] = kc_ref[0]
    vv[0:ATTN_CHUNK, :] = vp_ref[0]
    vv[ATTN_CHUNK:, :] = vc_ref[0]

    head0 = lax.broadcasted_iota(jnp.int32, (ATTN_BLK, LANES), 1) < HEAD_DIM
    qi = lax.broadcasted_iota(jnp.int32, (ATTN_BLK, 2 * ATTN_BLK), 0)
    kj = lax.broadcasted_iota(jnp.int32, (ATTN_BLK, 2 * ATTN_BLK), 1)
    dist = qi + ATTN_BLK - kj
    band = jnp.logical_and(dist >= 0, dist <= ATTN_BLK)
    neg_inf = jnp.float32(-jnp.inf)
    bias_band = jnp.where(band, 0.0, neg_inf)
    bias_first = jnp.where(jnp.logical_and(band, kj >= ATTN_BLK), 0.0, neg_inf)

    def pair(col0, col1):
        return jnp.where(head0, col0, col1)

    for pi, d in enumerate(DILATIONS):
        span = ATTN_BLK * d

        def body(idx, carry, d=d, span=span, first_pattern=(pi == 0)):
            u = idx // d
            r = idx % d
            q_start = u * span + r
            k_start = ATTN_CHUNK + q_start - span
            qb = q_ref[0, pl.ds(q_start, ATTN_BLK, stride=d), :]
            kb = kk[pl.ds(k_start, 2 * ATTN_BLK, stride=d), :].astype(BF16)
            vb = vv[pl.ds(k_start, 2 * ATTN_BLK, stride=d), :].astype(BF16)
            no_prev = jnp.logical_and(chunk == 0, u == 0)
            bias = jnp.where(no_prev, bias_first, bias_band)

            scores, mb = [], []
            for hsel in (True, False):
                qh = jnp.where(head0 == hsel, qb, 0.0).astype(BF16)
                s = lax.dot_general(qh, kb, (((1,), (1,)), ((), ())), preferred_element_type=F32) + bias
                scores.append(s)
                mb.append(jnp.max(s, axis=1, keepdims=True))

            rows = pl.ds(q_start, ATTN_BLK, stride=d)
            if first_pattern:
                mcol = mb
                m_new = pair(*mb)
            else:
                m_old = m_s[rows, :]
                mcol = [jnp.maximum(m_old[:, 0:1], mb[0]), jnp.maximum(m_old[:, HEAD_DIM:HEAD_DIM + 1], mb[1])]
                m_new = pair(*mcol)
                alpha = jnp.exp(m_old - m_new)

            lb, pv = [], []
            for hidx in range(2):
                p = jnp.exp(scores[hidx] - mcol[hidx])
                lb.append(jnp.sum(p, axis=1, keepdims=True))
                pv.append(jnp.dot(p.astype(BF16), vb, preferred_element_type=F32))
            l_new = pair(*lb)
            acc_new = jnp.where(head0, pv[0], pv[1])
            if not first_pattern:
                l_new = l_new + alpha * l_s[rows, :]
                acc_new = acc_new + alpha * acc_s[rows, :]
            m_s[rows, :] = m_new
            l_s[rows, :] = l_new
            acc_s[rows, :] = acc_new
            return carry

        lax.fori_loop(0, ATTN_CHUNK // ATTN_BLK, body, 0)

    o_ref[0] = (acc_s[...] / l_s[...]).astype(o_ref.dtype)


def _mix_kernel(co_ref, ao_ref, x_ref, mod_ref, wo_ref, gffn_ref, wr_ref, br_ref, x1_ref, h2_ref, lg_ref):
    rows = x_ref.shape[1]
    mixed = (jnp.dot(co_ref[0], wo_ref[0:CONV_CH, :], preferred_element_type=F32)
             + jnp.dot(ao_ref[0], wo_ref[CONV_CH:, :], preferred_element_type=F32))
    x1 = x_ref[0] + mod_ref[0, 2:3, :] * mixed
    x1_ref[0] = x1
    h2 = _rms_modulate(x1, gffn_ref[...], mod_ref[0, 3:4, :], mod_ref[0, 4:5, :])
    for j in range(SLABS):
        h2_ref[pl.ds(j, rows, stride=SLABS), :] = h2[:, j * LANES:(j + 1) * LANES]
    lg_ref[...] = jnp.dot(h2, wr_ref[...], preferred_element_type=F32, precision=HIGHEST) + br_ref[...]


def _slab_rows(idx):
    return pl.ds(pl.multiple_of(idx * SLABS, SLABS), SLABS)


def _expert_kernel(meta_ref, idx_hbm, h2_hbm, wgu_ref, bgu_ref, wd_ref, bd_ref, y_hbm,
                   xbuf, ybuf, idx_a, idx_b, wgu_bf, wd_bf, gsem, ssem, isem):
    i = pl.program_id(0)
    n_used = meta_ref[0]
    expert = meta_ref[1 + i]
    prev_expert = meta_ref[jnp.maximum(i, 1)]
    cur = i % 2
    idx_bufs = (idx_a, idx_b)
    idx_row = 2 * MOE_ROWS

    def idx_copy(row, slot):
        start = pl.multiple_of(row * idx_row, idx_row)
        return pltpu.make_async_copy(idx_hbm.at[pl.ds(start, idx_row)], idx_bufs[slot], isem)

    def row_slab(r):
        return pl.ds(r * SLABS, SLABS) if isinstance(r, int) else _slab_rows(r)

    def gather_copy(idx_slot, buf_slot, r):
        return pltpu.make_async_copy(h2_hbm.at[_slab_rows(idx_bufs[idx_slot][r]), :],
                                     xbuf.at[buf_slot, row_slab(r), :], gsem)

    def scatter_copy(idx_slot, buf_slot, r):
        return pltpu.make_async_copy(ybuf.at[buf_slot, row_slab(r), :],
                                     y_hbm.at[_slab_rows(idx_bufs[idx_slot][MOE_ROWS + r]), :], ssem)

    def start_rows(idx_slot, buf_slot, gather, scatter, unrolled):
        def body(r, carry):
            if gather:
                gather_copy(idx_slot, buf_slot, r).start()
            if scatter:
                scatter_copy(idx_slot, buf_slot, r).start()
            return carry
        if unrolled:
            for r in range(MOE_ROWS):
                body(r, 0)
        else:
            lax.fori_loop(0, MOE_ROWS, body, 0, unroll=8)

    def wait_gathers():
        pltpu.make_async_copy(h2_hbm.at[pl.ds(0, MOE_ROWS * SLABS), :], xbuf.at[0], gsem).wait()

    def wait_scatters():
        pltpu.make_async_copy(ybuf.at[0], y_hbm.at[pl.ds(0, MOE_ROWS * SLABS), :], ssem).wait()

    @pl.when(i == 0)
    def _():
        first = idx_copy(0, 1)
        first.start()
        first.wait()
        start_rows(1, 0, True, False, False)
        wait_gathers()
        ybuf[1] = jnp.zeros((MOE_ROWS * SLABS, LANES), F32)
        second = idx_copy(1, 0)
        second.start()
        second.wait()

    @pl.when(jnp.logical_or(i == 0, expert != prev_expert))
    def _():
        wgu_bf[...] = wgu_ref[0].astype(BF16)
        wd_bf[...] = wd_ref[0].astype(BF16)

    for parity in (0, 1):
        @pl.when(jnp.logical_and(i < n_used, cur == parity))
        def _(parity=parity):
            idx_copy(i + 2, 1 - parity).start()
            start_rows(parity, 1 - parity, True, True, True)

        @pl.when(jnp.logical_and(i == n_used, cur == parity))
        def _(parity=parity):
            start_rows(parity, 1 - parity, False, True, False)
            wait_scatters()

    @pl.when(i < n_used)
    def _():
        x = jnp.concatenate([xbuf[cur, pl.ds(j, MOE_ROWS, stride=SLABS), :] for j in range(SLABS)],
                            axis=1).astype(BF16)
        y = jnp.broadcast_to(bd_ref[0], (MOE_ROWS, D_MODEL))
        for cidx in range(D_FF // MOE_COLS):
            lo = cidx * MOE_COLS
            gate = jnp.dot(x, wgu_bf[:, lo:lo + MOE_COLS], preferred_element_type=F32) + bgu_ref[0, :, lo:lo + MOE_COLS]
            up = (jnp.dot(x, wgu_bf[:, D_FF + lo:D_FF + lo + MOE_COLS], preferred_element_type=F32)
                  + bgu_ref[0, :, D_FF + lo:D_FF + lo + MOE_COLS])
            gate = jnp.minimum(gate, SWIGLU_LIMIT)
            up = jnp.clip(up, -SWIGLU_LIMIT, SWIGLU_LIMIT)
            act = (up + 1.0) * (gate * jax.nn.sigmoid(SWIGLU_ALPHA * gate))
            y = y + jnp.dot(act.astype(BF16), wd_bf[lo:lo + MOE_COLS, :], preferred_element_type=F32)
        for j in range(SLABS):
            ybuf[cur, pl.ds(j, MOE_ROWS, stride=SLABS), :] = y[:, j * LANES:(j + 1) * LANES]
        wait_gathers()
        wait_scatters()
        idx_copy(0, 0).wait()


def _combine_kernel(x1_ref, gate_ref, mod_ref, gfin_ref, y_ref, o_ref):
    gates = gate_ref[...]
    stride = TOP_K * SLABS
    cols = []
    for j in range(SLABS):
        acc = gates[:, 0:1] * y_ref[pl.ds(j, TOK_TILE, stride=stride), :]
        for k in range(1, TOP_K):
            acc = acc + gates[:, k:k + 1] * y_ref[pl.ds(k * SLABS + j, TOK_TILE, stride=stride), :]
        cols.append(acc)
    ffn = jnp.concatenate(cols, axis=1)
    x2 = x1_ref[...] + mod_ref[0, 5:6, :] * ffn
    ms = jnp.mean(x2 * x2, axis=-1, keepdims=True)
    o_ref[...] = x2 * lax.rsqrt(ms + NORM_EPS) * gfin_ref[...]


def _rotary_lane_tables(positions):
    half = ROT_DIM // 2
    inv_freq = ROPE_THETA ** (-jnp.arange(0, ROT_DIM, 2, dtype=F32) / ROT_DIM)
    ang = positions.astype(F32)[..., None] * inv_freq
    cos, sin = jnp.cos(ang), jnp.sin(ang)
    ones = jnp.ones(cos.shape[:-1] + (HEAD_DIM - ROT_DIM,), F32)
    zeros = jnp.zeros_like(ones)
    zero_h = jnp.zeros_like(sin)
    rc = jnp.concatenate([cos, cos, ones], axis=-1)
    rsa = jnp.concatenate([-sin, zero_h, zeros], axis=-1)
    rsb = jnp.concatenate([zero_h, sin, zeros], axis=-1)
    reps = LANES // HEAD_DIM
    return tuple(jnp.tile(t, (1, 1, reps)) for t in (rc, rsa, rsb))


def _routing(logits):
    n_tok = logits.shape[0]
    n_asg = n_tok * TOP_K
    i32 = jnp.int32
    top_val, top_idx = lax.top_k(logits, TOP_K)
    gates = jax.nn.softmax(top_val, axis=-1)
    key = top_idx.reshape(n_asg).astype(i32) * n_asg + jnp.arange(n_asg, dtype=i32)
    skey = lax.sort(key)
    asg_sorted = skey % n_asg
    starts = jnp.searchsorted(skey, jnp.arange(N_EXPERTS + 1, dtype=i32) * n_asg, side="left").astype(i32)
    counts = starts[1:] - starts[:-1]
    padded = (counts + MOE_ROWS - 1) // MOE_ROWS * MOE_ROWS
    pends = jnp.cumsum(padded)
    pstarts = pends - padded
    n_blocks = n_asg // MOE_ROWS + N_EXPERTS
    n_used = (pends[-1] // MOE_ROWS).astype(i32)
    blk = jnp.arange(n_blocks + 2, dtype=i32)
    blk_start = blk * MOE_ROWS
    clamped = jnp.minimum(blk_start, pends[-1] - MOE_ROWS)
    blk_expert = jnp.minimum(jnp.searchsorted(pends, clamped, side="right"), N_EXPERTS - 1).astype(i32)
    row = jnp.arange(MOE_ROWS, dtype=i32)[None, :]
    off = blk_start[:, None] + row - pstarts[blk_expert][:, None]
    valid = jnp.logical_and(off < counts[blk_expert][:, None], blk[:, None] < n_used)
    asg = asg_sorted[jnp.clip(starts[blk_expert][:, None] + off, 0, n_asg - 1)]
    tok = jnp.where(valid, asg // TOP_K, 0)
    spare = jnp.broadcast_to(n_asg + row, (2, MOE_ROWS))
    slot = jnp.concatenate([spare, jnp.where(valid, asg, n_asg + row)[:-2]], axis=0)
    idx = jnp.concatenate([tok, slot], axis=1).reshape(-1).astype(i32)
    meta = jnp.concatenate([n_used[None], blk_expert[:n_blocks + 1]])
    return gates, idx, meta, n_blocks


def kernel(x, c, positions, w_ada, b_ada, g_mix, w_in, conv_w, conv_b, conv_ln_g, conv_ln_b,
           w_out, g_ffn, w_router, b_router, w_gu, b_gu, w_down, b_down, g_final):
    bsz, seq, d = x.shape
    n_tok = bsz * seq
    depth = w_ada.shape[0]
    assert depth == 1, "the final RMSNorm is fused into the layer's combine kernel"
    rc, rsa, rsb = _rotary_lane_tables(positions)
    s_tiles = seq // PROJ_ROWS
    arb2 = pltpu.CompilerParams(dimension_semantics=("arbitrary", "arbitrary"), vmem_limit_bytes=VMEM_LIMIT)
    arb1 = pltpu.CompilerParams(dimension_semantics=("arbitrary",), vmem_limit_bytes=VMEM_LIMIT)

    for l in range(depth):
        n_mod = 6 * d
        mod = pl.pallas_call(
            _ada_kernel,
            grid=(n_mod // d,),
            in_specs=[pl.BlockSpec((bsz, d), lambda j: (0, 0)),
                      pl.BlockSpec((d, d), lambda j: (0, j)),
                      pl.BlockSpec((1, d), lambda j: (0, j))],
            out_specs=pl.BlockSpec((bsz, d), lambda j: (0, j)),
            out_shape=jax.ShapeDtypeStruct((bsz, n_mod), F32),
            name="ada",
        )(c, w_ada[l], b_ada[l].reshape(1, n_mod))
        mod = mod.reshape(bsz, 6, d)

        row_spec = lambda width: pl.BlockSpec((1, PROJ_ROWS, width), lambda b, s: (b, s, 0))
        full2 = lambda shape: pl.BlockSpec(shape, lambda b, s: (0, 0))
        mod_spec = pl.BlockSpec((1, 6, d), lambda b, s: (b, 0, 0))
        conv_out, q, k, v = pl.pallas_call(
            _proj_kernel,
            grid=(bsz, s_tiles),
            in_specs=[row_spec(d), mod_spec, full2((1, d)), full2((d, IN_COLS)),
                      full2((CONV_WIDTH, CONV_CH)), full2((1, CONV_CH)), full2((1, CONV_CH)), full2((1, CONV_CH)),
                      row_spec(LANES), row_spec(LANES), row_spec(LANES)],
            out_specs=[row_spec(CONV_CH), row_spec(ATTN_WIDTH), row_spec(ATTN_WIDTH), row_spec(ATTN_WIDTH)],
            out_shape=[jax.ShapeDtypeStruct((bsz, seq, CONV_CH), BF16),
                       jax.ShapeDtypeStruct((bsz, seq, ATTN_WIDTH), F32),
                       jax.ShapeDtypeStruct((bsz, seq, ATTN_WIDTH), F32),
                       jax.ShapeDtypeStruct((bsz, seq, ATTN_WIDTH), F32)],
            scratch_shapes=[pltpu.VMEM((PROJ_ROWS + CONV_HALO, CONV_CH), F32)],
            compiler_params=arb2,
            name="proj",
        )(x, mod, g_mix[l].reshape(1, d), w_in[l].astype(BF16), conv_w[l].reshape(CONV_WIDTH, CONV_CH),
          conv_b[l].reshape(1, CONV_CH), conv_ln_g[l].reshape(1, CONV_CH), conv_ln_b[l].reshape(1, CONV_CH),
          rc, rsa, rsb)

        cur = pl.BlockSpec((1, ATTN_CHUNK, LANES), lambda b, p, ch: (b, ch, p))
        prev = pl.BlockSpec((1, ATTN_CHUNK, LANES), lambda b, p, ch: (b, jnp.maximum(ch - 1, 0), p))
        attn_out = pl.pallas_call(
            _attn_kernel,
            grid=(bsz, ATTN_WIDTH // LANES, seq // ATTN_CHUNK),
            in_specs=[cur, prev, cur, prev, cur],
            out_specs=cur,
            out_shape=jax.ShapeDtypeStruct((bsz, seq, ATTN_WIDTH), BF16),
            scratch_shapes=[pltpu.VMEM((2 * ATTN_CHUNK, LANES), F32), pltpu.VMEM((2 * ATTN_CHUNK, LANES), F32),
                            pltpu.VMEM((ATTN_CHUNK, LANES), F32), pltpu.VMEM((ATTN_CHUNK, LANES), F32),
                            pltpu.VMEM((ATTN_CHUNK, LANES), F32)],
            compiler_params=pltpu.CompilerParams(dimension_semantics=("arbitrary",) * 3,
                                                 vmem_limit_bytes=VMEM_LIMIT),
            name="attn",
        )(q, k, k, v, v)

        x1, h2, logits = pl.pallas_call(
            _mix_kernel,
            grid=(bsz, s_tiles),
            in_specs=[row_spec(CONV_CH), row_spec(ATTN_WIDTH), row_spec(d), mod_spec, full2((d, d)),
                      full2((1, d)), full2((d, N_EXPERTS)), full2((1, N_EXPERTS))],
            out_specs=[row_spec(d),
                       pl.BlockSpec((PROJ_ROWS * SLABS, LANES), lambda b, s: (b * s_tiles + s, 0)),
                       pl.BlockSpec((PROJ_ROWS, N_EXPERTS), lambda b, s: (b * s_tiles + s, 0))],
            out_shape=[jax.ShapeDtypeStruct((bsz, seq, d), F32),
                       jax.ShapeDtypeStruct((n_tok * SLABS, LANES), F32),
                       jax.ShapeDtypeStruct((n_tok, N_EXPERTS), F32)],
            compiler_params=arb2,
            name="mix",
        )(conv_out, attn_out, x, mod, w_out[l].astype(BF16), g_ffn[l].reshape(1, d),
          w_router[l], b_router[l].reshape(1, N_EXPERTS))

        gates, idx, meta, n_blocks = _routing(logits)
        n_slots = n_tok * TOP_K + MOE_ROWS
        n_tiles = n_tok // TOK_TILE

        def expert_block(i, meta):
            return (meta[1 + i], 0, 0)

        any_spec = pl.BlockSpec(memory_space=pl.ANY)
        y = pl.pallas_call(
            _expert_kernel,
            grid_spec=pltpu.PrefetchScalarGridSpec(
                num_scalar_prefetch=1,
                grid=(n_blocks + 1,),
                in_specs=[any_spec, any_spec,
                          pl.BlockSpec((1, d, 2 * D_FF), expert_block),
                          pl.BlockSpec((1, 1, 2 * D_FF), expert_block),
                          pl.BlockSpec((1, D_FF, d), expert_block),
                          pl.BlockSpec((1, 1, d), expert_block)],
                out_specs=any_spec,
                scratch_shapes=[pltpu.VMEM((2, MOE_ROWS * SLABS, LANES), F32),
                                pltpu.VMEM((2, MOE_ROWS * SLABS, LANES), F32),
                                pltpu.SMEM((2 * MOE_ROWS,), jnp.int32), pltpu.SMEM((2 * MOE_ROWS,), jnp.int32),
                                pltpu.VMEM((d, 2 * D_FF), BF16), pltpu.VMEM((D_FF, d), BF16),
                                pltpu.SemaphoreType.DMA(()), pltpu.SemaphoreType.DMA(()),
                                pltpu.SemaphoreType.DMA(())]),
            out_shape=jax.ShapeDtypeStruct((n_slots * SLABS, LANES), F32),
            compiler_params=arb1,
            name="experts",
        )(meta, idx, h2, w_gu[l], b_gu[l].reshape(N_EXPERTS, 1, 2 * D_FF), w_down[l],
          b_down[l].reshape(N_EXPERTS, 1, d))

        out = pl.pallas_call(
            _combine_kernel,
            grid=(n_tiles,),
            in_specs=[pl.BlockSpec((TOK_TILE, d), lambda i: (i, 0)),
                      pl.BlockSpec((TOK_TILE, TOP_K), lambda i: (i, 0)),
                      pl.BlockSpec((1, 6, d), lambda i: (i // (seq // TOK_TILE), 0, 0)),
                      pl.BlockSpec((1, d), lambda i: (0, 0)),
                      pl.BlockSpec((TOK_TILE * TOP_K * SLABS, LANES), lambda i: (i, 0))],
            out_specs=pl.BlockSpec((TOK_TILE, d), lambda i: (i, 0)),
            out_shape=jax.ShapeDtypeStruct((n_tok, d), F32),
            compiler_params=arb1,
            name="combine",
        )(x1.reshape(n_tok, d), gates, mod, g_final.reshape(1, d), y)
        x = out.reshape(bsz, seq, d)
    return x
```

```python
import functools

import jax
import jax.numpy as jnp
from jax import lax
from jax.experimental import pallas as pl
from jax.experimental.pallas import tpu as pltpu

F32 = jnp.float32
BF16 = jnp.bfloat16
HIGHEST = lax.Precision.HIGHEST

D_MODEL = 1024
N_HEADS = 8
HEAD_DIM = 64
ATTN_WIDTH = N_HEADS * HEAD_DIM
CONV_CH = D_MODEL - ATTN_WIDTH
IN_COLS = 2 * CONV_CH + 3 * ATTN_WIDTH
CONV_WIDTH = 31
ROT_DIM = HEAD_DIM // 4
ROPE_THETA = 500000.0
DILATIONS = (1, 4, 16)
ATTN_BLK = 128
N_EXPERTS = 32
TOP_K = 4
D_FF = D_MODEL
SWIGLU_LIMIT = 7.0
SWIGLU_ALPHA = 1.702
NORM_EPS = 1e-6

LANES = 128
SUBLANES = 8
SLABS = D_MODEL // LANES
VMEM_LIMIT = 56 * 1024 * 1024

PROJ_ROWS = 512
CONV_HALO = 32
CONV_ROWS = 64
ATTN_CHUNK = 2048
MOE_ROWS = 512
MOE_COLS = 256
TOK_TILE = 256


def _ada_kernel(c_ref, w_ref, b_ref, o_ref):
    c = c_ref[...]
    c_act = c * jax.nn.sigmoid(c)
    o_ref[...] = jnp.dot(c_act, w_ref[...], preferred_element_type=F32, precision=HIGHEST) + b_ref[...]


def _rms_modulate(x, gain, shift, scale):
    ms = jnp.mean(x * x, axis=-1, keepdims=True)
    y = x * lax.rsqrt(ms + NORM_EPS) * gain
    return y * (1.0 + scale) + shift


def _proj_kernel(x_ref, mod_ref, gmix_ref, win_ref, cw_ref, cb_ref, lng_ref, lnb_ref,
                 rc_ref, rsa_ref, rsb_ref, co_ref, q_ref, k_ref, v_ref, ubuf):
    rows = x_ref.shape[1]
    h = _rms_modulate(x_ref[0], gmix_ref[...], mod_ref[0, 0:1, :], mod_ref[0, 1:2, :])
    proj = jnp.dot(h.astype(BF16), win_ref[...], preferred_element_type=F32)

    u = proj[:, :CONV_CH] * jax.nn.sigmoid(proj[:, CONV_CH:2 * CONV_CH])

    @pl.when(pl.program_id(1) == 0)
    def _():
        ubuf[0:CONV_HALO, :] = jnp.zeros((CONV_HALO, CONV_CH), F32)

    ubuf[CONV_HALO:, :] = u
    first_tap = CONV_HALO - (CONV_WIDTH - 1)
    for ch in range(rows // CONV_ROWS):
        base = ch * CONV_ROWS + first_tap
        acc = jnp.broadcast_to(cb_ref[...], (CONV_ROWS, CONV_CH))
        for j in range(CONV_WIDTH):
            acc = acc + cw_ref[j:j + 1, :] * ubuf[base + j:base + j + CONV_ROWS, :]
        mu = jnp.mean(acc, axis=-1, keepdims=True)
        cen = acc - mu
        var = jnp.mean(cen * cen, axis=-1, keepdims=True)
        yn = cen * lax.rsqrt(var + NORM_EPS) * lng_ref[...] + lnb_ref[...]
        co_ref[0, ch * CONV_ROWS:(ch + 1) * CONV_ROWS, :] = (yn * jax.nn.sigmoid(yn)).astype(co_ref.dtype)
    ubuf[0:CONV_HALO, :] = ubuf[rows:rows + CONV_HALO, :]

    rc, rsa, rsb = rc_ref[0], rsa_ref[0], rsb_ref[0]
    half = ROT_DIM // 2
    q0 = 2 * CONV_CH
    k0 = q0 + ATTN_WIDTH
    v0 = k0 + ATTN_WIDTH
    for blk in range(ATTN_WIDTH // LANES):
        lo, hi = blk * LANES, (blk + 1) * LANES
        for src0, dst_ref, scale in ((q0, q_ref, HEAD_DIM ** -0.5), (k0, k_ref, 1.0)):
            t = proj[:, src0 + lo:src0 + hi]
            rot = t * rc + pltpu.roll(t, LANES - half, 1) * rsa + pltpu.roll(t, half, 1) * rsb
            dst_ref[0, :, lo:hi] = rot * scale
    v_ref[0] = proj[:, v0:v0 + ATTN_WIDTH]


def _attn_kernel(q_ref, kp_ref, kc_ref, vp_ref, vc_ref, o_ref, kk, vv, m_s, l_s, acc_s):
    chunk = pl.program_id(2)
    kk[0:ATTN_CHUNK, :] = kp_ref[0]
    kk[ATTN_CHUNK:, :] = kc_ref[0]
    vv[0:ATTN_CHUNK, :] = vp_ref[0]
    vv[ATTN_CHUNK:, :] = vc_ref[0]

    head0 = lax.broadcasted_iota(jnp.int32, (ATTN_BLK, LANES), 1) < HEAD_DIM
    qi = lax.broadcasted_iota(jnp.int32, (ATTN_BLK, 2 * ATTN_BLK), 0)
    kj = lax.broadcasted_iota(jnp.int32, (ATTN_BLK, 2 * ATTN_BLK), 1)
    dist = qi + ATTN_BLK - kj
    band = jnp.logical_and(dist >= 0, dist <= ATTN_BLK)
    neg_inf = jnp.float32(-jnp.inf)
    bias_band = jnp.where(band, 0.0, neg_inf)
    bias_first = jnp.where(jnp.logical_and(band, kj >= ATTN_BLK), 0.0, neg_inf)

    def pair(col0, col1):
        return jnp.where(head0, col0, col1)

    for pi, d in enumerate(DILATIONS):
        span = ATTN_BLK * d

        def body(idx, carry, d=d, span=span, first_pattern=(pi == 0)):
            u = idx // d
            r = idx % d
            q_start = u * span + r
            k_start = ATTN_CHUNK + q_start - span
            qb = q_ref[0, pl.ds(q_start, ATTN_BLK, stride=d), :]
            kb = kk[pl.ds(k_start, 2 * ATTN_BLK, stride=d), :].astype(BF16)
            vb = vv[pl.ds(k_start, 2 * ATTN_BLK, stride=d), :].astype(BF16)
            no_prev = jnp.logical_and(chunk == 0, u == 0)
            bias = jnp.where(no_prev, bias_first, bias_band)

            scores, mb = [], []
            for hsel in (True, False):
                qh = jnp.where(head0 == hsel, qb, 0.0).astype(BF16)
                s = lax.dot_general(qh, kb, (((1,), (1,)), ((), ())), preferred_element_type=F32) + bias
                scores.append(s)
                mb.append(jnp.max(s, axis=1, keepdims=True))

            rows = pl.ds(q_start, ATTN_BLK, stride=d)
            if first_pattern:
                mcol = mb
                m_new = pair(*mb)
            else:
                m_old = m_s[rows, :]
                mcol = [jnp.maximum(m_old[:, 0:1], mb[0]), jnp.maximum(m_old[:, HEAD_DIM:HEAD_DIM + 1], mb[1])]
                m_new = pair(*mcol)
                alpha = jnp.exp(m_old - m_new)

            lb, pv = [], []
            for hidx in range(2):
                p = jnp.exp(scores[hidx] - mcol[hidx])
                lb.append(jnp.sum(p, axis=1, keepdims=True))
                pv.append(jnp.dot(p.astype(BF16), vb, preferred_element_type=F32))
            l_new = pair(*lb)
            acc_new = jnp.where(head0, pv[0], pv[1])
            if not first_pattern:
                l_new = l_new + alpha * l_s[rows, :]
                acc_new = acc_new + alpha * acc_s[rows, :]
            m_s[rows, :] = m_new
            l_s[rows, :] = l_new
            acc_s[rows, :] = acc_new
            return carry

        lax.fori_loop(0, ATTN_CHUNK // ATTN_BLK, body, 0)

    o_ref[0] = (acc_s[...] / l_s[...]).astype(o_ref.dtype)


def _mix_kernel(co_ref, ao_ref, x_ref, mod_ref, wo_ref, gffn_ref, wr_ref, br_ref, x1_ref, h2_ref, lg_ref):
    rows = x_ref.shape[1]
    mixed = (jnp.dot(co_ref[0], wo_ref[0:CONV_CH, :], preferred_element_type=F32)
             + jnp.dot(ao_ref[0], wo_ref[CONV_CH:, :], preferred_element_type=F32))
    x1 = x_ref[0] + mod_ref[0, 2:3, :] * mixed
    x1_ref[0] = x1
    h2 = _rms_modulate(x1, gffn_ref[...], mod_ref[0, 3:4, :], mod_ref[0, 4:5, :])
    for j in range(SLABS):
        h2_ref[pl.ds(j, rows, stride=SLABS), :] = h2[:, j * LANES:(j + 1) * LANES]
    lg_ref[...] = jnp.dot(h2, wr_ref[...], preferred_element_type=F32, precision=HIGHEST) + br_ref[...]


def _slab_rows(idx):
    return pl.ds(pl.multiple_of(idx * SLABS, SLABS), SLABS)


def _expert_kernel(meta_ref, idx_hbm, h2_hbm, wgu_ref, bgu_ref, wd_ref, bd_ref, y_hbm,
                   xbuf, ybuf, idx_a, idx_b, wgu_bf, wd_bf, gsem, ssem, isem):
    i = pl.program_id(0)
    n_used = meta_ref[0]
    expert = meta_ref[1 + i]
    prev_expert = meta_ref[jnp.maximum(i, 1)]
    cur = i % 2
    idx_bufs = (idx_a, idx_b)
    idx_row = 2 * MOE_ROWS

    def idx_copy(row, slot):
        start = pl.multiple_of(row * idx_row, idx_row)
        return pltpu.make_async_copy(idx_hbm.at[pl.ds(start, idx_row)], idx_bufs[slot], isem)

    def row_slab(r):
        return pl.ds(r * SLABS, SLABS) if isinstance(r, int) else _slab_rows(r)

    def gather_copy(idx_slot, buf_slot, r):
        return pltpu.make_async_copy(h2_hbm.at[_slab_rows(idx_bufs[idx_slot][r]), :],
                                     xbuf.at[buf_slot, row_slab(r), :], gsem)

    def scatter_copy(idx_slot, buf_slot, r):
        return pltpu.make_async_copy(ybuf.at[buf_slot, row_slab(r), :],
                                     y_hbm.at[_slab_rows(idx_bufs[idx_slot][MOE_ROWS + r]), :], ssem)

    def start_rows(idx_slot, buf_slot, gather, scatter, unrolled):
        def body(r, carry, priority=0):
            if gather:
                gather_copy(idx_slot, buf_slot, r).start(priority=priority)
            if scatter:
                scatter_copy(idx_slot, buf_slot, r).start(priority=priority)
            return carry
        if unrolled:
            for r in range(MOE_ROWS):
                body(r, 0, r % 2)
        else:
            lax.fori_loop(0, MOE_ROWS, body, 0, unroll=8)

    def wait_gathers():
        pltpu.make_async_copy(h2_hbm.at[pl.ds(0, MOE_ROWS * SLABS), :], xbuf.at[0], gsem).wait()

    def wait_scatters():
        pltpu.make_async_copy(ybuf.at[0], y_hbm.at[pl.ds(0, MOE_ROWS * SLABS), :], ssem).wait()

    @pl.when(i == 0)
    def _():
        first = idx_copy(0, 1)
        first.start()
        first.wait()
        start_rows(1, 0, True, False, False)
        wait_gathers()
        ybuf[1] = jnp.zeros((MOE_ROWS * SLABS, LANES), F32)
        second = idx_copy(1, 0)
        second.start()
        second.wait()

    @pl.when(jnp.logical_or(i == 0, expert != prev_expert))
    def _():
        wgu_bf[...] = wgu_ref[0].astype(BF16)
        wd_bf[...] = wd_ref[0].astype(BF16)

    for parity in (0, 1):
        @pl.when(jnp.logical_and(i < n_used, cur == parity))
        def _(parity=parity):
            idx_copy(i + 2, 1 - parity).start()
            start_rows(parity, 1 - parity, True, True, True)

        @pl.when(jnp.logical_and(i == n_used, cur == parity))
        def _(parity=parity):
            start_rows(parity, 1 - parity, False, True, False)
            wait_scatters()

    @pl.when(i < n_used)
    def _():
        x = jnp.concatenate([xbuf[cur, pl.ds(j, MOE_ROWS, stride=SLABS), :] for j in range(SLABS)],
                            axis=1).astype(BF16)
        y = jnp.broadcast_to(bd_ref[0], (MOE_ROWS, D_MODEL))
        for cidx in range(D_FF // MOE_COLS):
            lo = cidx * MOE_COLS
            gate = jnp.dot(x, wgu_bf[:, lo:lo + MOE_COLS], preferred_element_type=F32) + bgu_ref[0, :, lo:lo + MOE_COLS]
            up = (jnp.dot(x, wgu_bf[:, D_FF + lo:D_FF + lo + MOE_COLS], preferred_element_type=F32)
                  + bgu_ref[0, :, D_FF + lo:D_FF + lo + MOE_COLS])
            gate = jnp.minimum(gate, SWIGLU_LIMIT)
            up = jnp.clip(up, -SWIGLU_LIMIT, SWIGLU_LIMIT)
            act = (up + 1.0) * (gate * jax.nn.sigmoid(SWIGLU_ALPHA * gate))
            y = y + jnp.dot(act.astype(BF16), wd_bf[lo:lo + MOE_COLS, :], preferred_element_type=F32)
        for j in range(SLABS):
            ybuf[cur, pl.ds(j, MOE_ROWS, stride=SLABS), :] = y[:, j * LANES:(j + 1) * LANES]
        wait_gathers()
        wait_scatters()
        idx_copy(0, 0).wait()


def _combine_kernel(x1_ref, gate_ref, mod_ref, gfin_ref, y_ref, o_ref):
    gates = gate_ref[...]
    stride = TOP_K * SLABS
    cols = []
    for j in range(SLABS):
        acc = gates[:, 0:1] * y_ref[pl.ds(j, TOK_TILE, stride=stride), :]
        for k in range(1, TOP_K):
            acc = acc + gates[:, k:k + 1] * y_ref[pl.ds(k * SLABS + j, TOK_TILE, stride=stride), :]
        cols.append(acc)
    ffn = jnp.concatenate(cols, axis=1)
    x2 = x1_ref[...] + mod_ref[0, 5:6, :] * ffn
    ms = jnp.mean(x2 * x2, axis=-1, keepdims=True)
    o_ref[...] = x2 * lax.rsqrt(ms + NORM_EPS) * gfin_ref[...]


def _rotary_lane_tables(positions):
    half = ROT_DIM // 2
    inv_freq = ROPE_THETA ** (-jnp.arange(0, ROT_DIM, 2, dtype=F32) / ROT_DIM)
    ang = inv_freq[:, None, None] * positions.astype(F32)[None]
    cos, sin = jnp.moveaxis(jnp.cos(ang), 0, -1), jnp.moveaxis(jnp.sin(ang), 0, -1)
    ones = jnp.ones(cos.shape[:-1] + (HEAD_DIM - ROT_DIM,), F32)
    zeros = jnp.zeros_like(ones)
    zero_h = jnp.zeros_like(sin)
    rc = jnp.concatenate([cos, cos, ones], axis=-1)
    rsa = jnp.concatenate([-sin, zero_h, zeros], axis=-1)
    rsb = jnp.concatenate([zero_h, sin, zeros], axis=-1)
    reps = LANES // HEAD_DIM
    return tuple(jnp.tile(t, (1, 1, reps)) for t in (rc, rsa, rsb))


def _routing(logits):
    n_tok = logits.shape[0]
    n_asg = n_tok * TOP_K
    i32 = jnp.int32
    top_val, top_idx = lax.top_k(logits, TOP_K)
    gates = jax.nn.softmax(top_val, axis=-1)
    key = top_idx.reshape(n_asg).astype(i32) * n_asg + jnp.arange(n_asg, dtype=i32)
    skey = lax.sort(key)
    asg_sorted = skey % n_asg
    experts = jnp.arange(N_EXPERTS, dtype=i32)
    counts = jnp.sum((top_idx.reshape(n_asg, 1) == experts[None, :]).astype(i32), axis=0)
    starts = jnp.cumsum(counts) - counts
    padded = (counts + MOE_ROWS - 1) // MOE_ROWS * MOE_ROWS
    pends = jnp.cumsum(padded)
    pstarts = pends - padded
    n_blocks = n_asg // MOE_ROWS + N_EXPERTS
    n_used = (pends[-1] // MOE_ROWS).astype(i32)
    blk = jnp.arange(n_blocks + 2, dtype=i32)
    blk_start = blk * MOE_ROWS
    clamped = jnp.minimum(blk_start, pends[-1] - MOE_ROWS)
    blk_expert = jnp.sum((pends[None, :] <= clamped[:, None]).astype(i32), axis=1)
    row = jnp.arange(MOE_ROWS, dtype=i32)[None, :]
    off = blk_start[:, None] + row - pstarts[blk_expert][:, None]
    valid = jnp.logical_and(off < counts[blk_expert][:, None], blk[:, None] < n_used)
    asg = asg_sorted[jnp.clip(starts[blk_expert][:, None] + off, 0, n_asg - 1)]
    tok = jnp.where(valid, asg // TOP_K, 0)
    spare = jnp.broadcast_to(n_asg + row, (2, MOE_ROWS))
    slot = jnp.concatenate([spare, jnp.where(valid, asg, n_asg + row)[:-2]], axis=0)
    idx = jnp.concatenate([tok, slot], axis=1).reshape(-1).astype(i32)
    meta = jnp.concatenate([n_used[None], blk_expert[:n_blocks + 1]])
    return gates, idx, meta, n_blocks


def kernel(x, c, positions, w_ada, b_ada, g_mix, w_in, conv_w, conv_b, conv_ln_g, conv_ln_b,
           w_out, g_ffn, w_router, b_router, w_gu, b_gu, w_down, b_down, g_final):
    bsz, seq, d = x.shape
    n_tok = bsz * seq
    depth = w_ada.shape[0]
    assert depth == 1, "the final RMSNorm is fused into the layer's combine kernel"
    rc, rsa, rsb = _rotary_lane_tables(positions)
    s_tiles = seq // PROJ_ROWS
    arb2 = pltpu.CompilerParams(dimension_semantics=("arbitrary", "arbitrary"), vmem_limit_bytes=VMEM_LIMIT)
    arb1 = pltpu.CompilerParams(dimension_semantics=("arbitrary",), vmem_limit_bytes=VMEM_LIMIT)

    for l in range(depth):
        n_mod = 6 * d
        mod = pl.pallas_call(
            _ada_kernel,
            grid=(n_mod // d,),
            in_specs=[pl.BlockSpec((bsz, d), lambda j: (0, 0)),
                      pl.BlockSpec((d, d), lambda j: (0, j)),
                      pl.BlockSpec((1, d), lambda j: (0, j))],
            out_specs=pl.BlockSpec((bsz, d), lambda j: (0, j)),
            out_shape=jax.ShapeDtypeStruct((bsz, n_mod), F32),
            name="ada",
        )(c, w_ada[l], b_ada[l].reshape(1, n_mod))
        mod = mod.reshape(bsz, 6, d)

        row_spec = lambda width: pl.BlockSpec((1, PROJ_ROWS, width), lambda b, s: (b, s, 0))
        full2 = lambda shape: pl.BlockSpec(shape, lambda b, s: (0, 0))
        mod_spec = pl.BlockSpec((1, 6, d), lambda b, s: (b, 0, 0))
        conv_out, q, k, v = pl.pallas_call(
            _proj_kernel,
            grid=(bsz, s_tiles),
            in_specs=[row_spec(d), mod_spec, full2((1, d)), full2((d, IN_COLS)),
                      full2((CONV_WIDTH, CONV_CH)), full2((1, CONV_CH)), full2((1, CONV_CH)), full2((1, CONV_CH)),
                      row_spec(LANES), row_spec(LANES), row_spec(LANES)],
            out_specs=[row_spec(CONV_CH), row_spec(ATTN_WIDTH), row_spec(ATTN_WIDTH), row_spec(ATTN_WIDTH)],
            out_shape=[jax.ShapeDtypeStruct((bsz, seq, CONV_CH), BF16),
                       jax.ShapeDtypeStruct((bsz, seq, ATTN_WIDTH), F32),
                       jax.ShapeDtypeStruct((bsz, seq, ATTN_WIDTH), F32),
                       jax.ShapeDtypeStruct((bsz, seq, ATTN_WIDTH), F32)],
            scratch_shapes=[pltpu.VMEM((PROJ_ROWS + CONV_HALO, CONV_CH), F32)],
            compiler_params=arb2,
            name="proj",
        )(x, mod, g_mix[l].reshape(1, d), w_in[l].astype(BF16), conv_w[l].reshape(CONV_WIDTH, CONV_CH),
          conv_b[l].reshape(1, CONV_CH), conv_ln_g[l].reshape(1, CONV_CH), conv_ln_b[l].reshape(1, CONV_CH),
          rc, rsa, rsb)

        cur = pl.BlockSpec((1, ATTN_CHUNK, LANES), lambda b, p, ch: (b, ch, p))
        prev = pl.BlockSpec((1, ATTN_CHUNK, LANES), lambda b, p, ch: (b, jnp.maximum(ch - 1, 0), p))
        attn_out = pl.pallas_call(
            _attn_kernel,
            grid=(bsz, ATTN_WIDTH // LANES, seq // ATTN_CHUNK),
            in_specs=[cur, prev, cur, prev, cur],
            out_specs=cur,
            out_shape=jax.ShapeDtypeStruct((bsz, seq, ATTN_WIDTH), BF16),
            scratch_shapes=[pltpu.VMEM((2 * ATTN_CHUNK, LANES), F32), pltpu.VMEM((2 * ATTN_CHUNK, LANES), F32),
                            pltpu.VMEM((ATTN_CHUNK, LANES), F32), pltpu.VMEM((ATTN_CHUNK, LANES), F32),
                            pltpu.VMEM((ATTN_CHUNK, LANES), F32)],
            compiler_params=pltpu.CompilerParams(dimension_semantics=("arbitrary",) * 3,
                                                 vmem_limit_bytes=VMEM_LIMIT),
            name="attn",
        )(q, k, k, v, v)

        x1, h2, logits = pl.pallas_call(
            _mix_kernel,
            grid=(bsz, s_tiles),
            in_specs=[row_spec(CONV_CH), row_spec(ATTN_WIDTH), row_spec(d), mod_spec, full2((d, d)),
                      full2((1, d)), full2((d, N_EXPERTS)), full2((1, N_EXPERTS))],
            out_specs=[row_spec(d),
                       pl.BlockSpec((PROJ_ROWS * SLABS, LANES), lambda b, s: (b * s_tiles + s, 0)),
                       pl.BlockSpec((PROJ_ROWS, N_EXPERTS), lambda b, s: (b * s_tiles + s, 0))],
            out_shape=[jax.ShapeDtypeStruct((bsz, seq, d), F32),
                       jax.ShapeDtypeStruct((n_tok * SLABS, LANES), F32),
                       jax.ShapeDtypeStruct((n_tok, N_EXPERTS), F32)],
            compiler_params=arb2,
            name="mix",
        )(conv_out, attn_out, x, mod, w_out[l].astype(BF16), g_ffn[l].reshape(1, d),
          w_router[l], b_router[l].reshape(1, N_EXPERTS))

        gates, idx, meta, n_blocks = _routing(logits)
        n_slots = n_tok * TOP_K + MOE_ROWS
        n_tiles = n_tok // TOK_TILE

        def expert_block(i, meta):
            return (meta[1 + i], 0, 0)

        any_spec = pl.BlockSpec(memory_space=pl.ANY)
        y = pl.pallas_call(
            _expert_kernel,
            grid_spec=pltpu.PrefetchScalarGridSpec(
                num_scalar_prefetch=1,
                grid=(n_blocks + 1,),
                in_specs=[any_spec, any_spec,
                          pl.BlockSpec((1, d, 2 * D_FF), expert_block),
                          pl.BlockSpec((1, 1, 2 * D_FF), expert_block),
                          pl.BlockSpec((1, D_FF, d), expert_block),
                          pl.BlockSpec((1, 1, d), expert_block)],
                out_specs=any_spec,
                scratch_shapes=[pltpu.VMEM((2, MOE_ROWS * SLABS, LANES), F32),
                                pltpu.VMEM((2, MOE_ROWS * SLABS, LANES), F32),
                                pltpu.SMEM((2 * MOE_ROWS,), jnp.int32), pltpu.SMEM((2 * MOE_ROWS,), jnp.int32),
                                pltpu.VMEM((d, 2 * D_FF), BF16), pltpu.VMEM((D_FF, d), BF16),
                                pltpu.SemaphoreType.DMA(()), pltpu.SemaphoreType.DMA(()),
                                pltpu.SemaphoreType.DMA(())]),
            out_shape=jax.ShapeDtypeStruct((n_slots * SLABS, LANES), F32),
            compiler_params=arb1,
            name="experts",
        )(meta, idx, h2, w_gu[l], b_gu[l].reshape(N_EXPERTS, 1, 2 * D_FF), w_down[l],
          b_down[l].reshape(N_EXPERTS, 1, d))

        out = pl.pallas_call(
            _combine_kernel,
            grid=(n_tiles,),
            in_specs=[pl.BlockSpec((TOK_TILE, d), lambda i: (i, 0)),
                      pl.BlockSpec((TOK_TILE, TOP_K), lambda i: (i, 0)),
                      pl.BlockSpec((1, 6, d), lambda i: (i // (seq // TOK_TILE), 0, 0)),
                      pl.BlockSpec((1, d), lambda i: (0, 0)),
                      pl.BlockSpec((TOK_TILE * TOP_K * SLABS, LANES), lambda i: (i, 0))],
            out_specs=pl.BlockSpec((TOK_TILE, d), lambda i: (i, 0)),
            out_shape=jax.ShapeDtypeStruct((n_tok, d), F32),
            compiler_params=arb1,
            name="combine",
        )(x1.reshape(n_tok, d), gates, mod, g_final.reshape(1, d), y)
        x = out.reshape(bsz, seq, d)
    return x
```

```python
import functools

import jax
import numpy as np
import jax.numpy as jnp
from jax import lax
from jax.experimental import pallas as pl
from jax.experimental.pallas import tpu as pltpu

F32 = jnp.float32
BF16 = jnp.bfloat16
HIGHEST = lax.Precision.HIGHEST

D_MODEL = 1024
N_HEADS = 8
HEAD_DIM = 64
ATTN_WIDTH = N_HEADS * HEAD_DIM
CONV_CH = D_MODEL - ATTN_WIDTH
IN_COLS = 2 * CONV_CH + 3 * ATTN_WIDTH
CONV_WIDTH = 31
ROT_DIM = HEAD_DIM // 4
ROPE_THETA = 500000.0
DILATIONS = (1, 4, 16)
ATTN_BLK = 128
ATTN_GROUP = 16
N_EXPERTS = 32
TOP_K = 4
D_FF = D_MODEL
SWIGLU_LIMIT = 7.0
SWIGLU_ALPHA = 1.702
NORM_EPS = 1e-6

LANES = 128
SUBLANES = 8
SLABS = D_MODEL // LANES
VMEM_LIMIT = 56 * 1024 * 1024

PROJ_ROWS = 512
CONV_HALO = 32
CONV_ROWS = 64
ATTN_CHUNK = 2048
MOE_ROWS = 512
MOE_COLS = 256
TOK_TILE = 256


def _ada_kernel(c_ref, w_ref, b_ref, o_ref):
    c = c_ref[...]
    c_act = c * jax.nn.sigmoid(c)
    o_ref[...] = jnp.dot(c_act, w_ref[...], preferred_element_type=F32, precision=HIGHEST) + b_ref[...]


def _rms_modulate(x, gain, shift, scale):
    ms = jnp.mean(x * x, axis=-1, keepdims=True)
    y = x * lax.rsqrt(ms + NORM_EPS) * gain
    return y * (1.0 + scale) + shift


def _proj_kernel(x_ref, mod_ref, gmix_ref, win_ref, perm_ref, cw_ref, cb_ref, lng_ref, lnb_ref,
                 rc_ref, rsa_ref, rsb_ref, co_ref, q_ref, k_ref, v_ref, ubuf):
    rows = x_ref.shape[1]
    hb = _rms_modulate(x_ref[0], gmix_ref[...], mod_ref[0, 0:1, :], mod_ref[0, 1:2, :]).astype(BF16)

    ag = jnp.dot(hb, win_ref[:, :2 * CONV_CH], preferred_element_type=F32)
    u = ag[:, :CONV_CH] * jax.nn.sigmoid(ag[:, CONV_CH:])

    @pl.when(pl.program_id(1) == 0)
    def _():
        ubuf[0:CONV_HALO, :] = jnp.zeros((CONV_HALO, CONV_CH), F32)

    ubuf[CONV_HALO:, :] = u
    first_tap = CONV_HALO - (CONV_WIDTH - 1)
    for ch in range(rows // CONV_ROWS):
        base = ch * CONV_ROWS + first_tap
        acc = jnp.broadcast_to(cb_ref[...], (CONV_ROWS, CONV_CH))
        for j in range(CONV_WIDTH):
            acc = acc + cw_ref[j:j + 1, :] * ubuf[base + j:base + j + CONV_ROWS, :]
        mu = jnp.mean(acc, axis=-1, keepdims=True)
        cen = acc - mu
        var = jnp.mean(cen * cen, axis=-1, keepdims=True)
        yn = cen * lax.rsqrt(var + NORM_EPS) * lng_ref[...] + lnb_ref[...]
        co_ref[0, ch * CONV_ROWS:(ch + 1) * CONV_ROWS, :] = (yn * jax.nn.sigmoid(yn)).astype(co_ref.dtype)
    ubuf[0:CONV_HALO, :] = ubuf[rows:rows + CONV_HALO, :]

    hp = jnp.dot(perm_ref[...], hb, preferred_element_type=F32).astype(BF16)
    qkv = jnp.dot(hp, win_ref[:, 2 * CONV_CH:], preferred_element_type=F32)
    run = rows // ATTN_GROUP
    tile = (ATTN_GROUP, run, LANES)
    rc, rsa, rsb = (t[0, 0].reshape(rows, LANES) for t in (rc_ref, rsa_ref, rsb_ref))
    lane = lax.broadcasted_iota(jnp.int32, (rows, LANES), 1)
    ones_col = jnp.where(lane == HEAD_DIM, 1.0, 0.0)
    half = ROT_DIM // 2
    for blk in range(ATTN_WIDTH // LANES):
        lo, hi = blk * LANES, (blk + 1) * LANES
        for src0, dst_ref, scale in ((0, q_ref, HEAD_DIM ** -0.5), (ATTN_WIDTH, k_ref, 1.0)):
            t = qkv[:, src0 + lo:src0 + hi]
            rot = t * rc + pltpu.roll(t, LANES - half, 1) * rsa + pltpu.roll(t, half, 1) * rsb
            dst_ref[0, 0, :, :, lo:hi] = (rot * scale).reshape(tile)
        vb = qkv[:, 2 * ATTN_WIDTH + lo:2 * ATTN_WIDTH + hi]
        for hsel, vals in enumerate((vb, pltpu.roll(vb, HEAD_DIM, 1))):
            vh = jnp.where(lane < HEAD_DIM, vals, ones_col)
            v_ref[0, 0, :, :, (2 * blk + hsel) * LANES:(2 * blk + hsel + 1) * LANES] = vh.reshape(tile)


def _attn_kernel(bias_ref, q_ref, kp_ref, kc_ref, vp_ref, vc_ref, o_ref, kk, vv, st):
    chunk = pl.program_id(2)
    kk[0] = kp_ref[0, 0]
    kk[1] = kc_ref[0, 0]
    vv[0] = vp_ref[0, 0]
    vv[1] = vc_ref[0, 0]
    lane = lax.broadcasted_iota(jnp.int32, (ATTN_BLK, LANES), 1)
    head0 = lane < HEAD_DIM
    kcol = lax.broadcasted_iota(jnp.int32, (ATTN_BLK, 2 * ATTN_BLK), 1)
    neg_inf = jnp.float32(-jnp.inf)
    n_blocks = ATTN_CHUNK // ATTN_BLK

    for pi, d in enumerate(DILATIONS):
        n_runs = ATTN_GROUP // d
        run = ATTN_BLK // n_runs
        n_sub = n_blocks // d
        bias = bias_ref[pi]
        bias_first = jnp.where(kcol >= ATTN_BLK, bias, neg_inf)

        def body(idx, carry, pi=pi, d=d, n_runs=n_runs, run=run, n_sub=n_sub, bias=bias, bias_first=bias_first):
            u = idx // d
            res = idx % d
            has_prev = u > 0
            a_cur = pl.multiple_of(u * run, run)
            a_prev = pl.multiple_of(jnp.where(has_prev, u - 1, n_sub - 1) * run, run)
            half_prev = jnp.where(has_prev, 1, 0)

            def rows_of(ref, lead, a0, lanes):
                parts = [ref[lead + (i * d + res, pl.ds(a0, run), lanes)] for i in range(n_runs)]
                return parts[0] if n_runs == 1 else jnp.concatenate(parts, axis=0)

            all_lanes = slice(None)
            qb = rows_of(q_ref, (0, 0), a_cur, all_lanes)
            kb = jnp.concatenate([rows_of(kk, (half_prev,), a_prev, all_lanes),
                                  rows_of(kk, (1,), a_cur, all_lanes)], axis=0).astype(BF16)
            q2 = jnp.concatenate([jnp.where(head0, qb, 0.0), jnp.where(head0, 0.0, qb)], axis=0).astype(BF16)
            no_prev = jnp.logical_and(chunk == 0, jnp.logical_not(has_prev))
            b = jnp.where(no_prev, bias_first, bias)
            s2 = lax.dot_general(q2, kb, (((1,), (1,)), ((), ())), preferred_element_type=F32)
            for hsel in range(2):
                s = s2[hsel * ATTN_BLK:(hsel + 1) * ATTN_BLK] + b
                m = jnp.max(s, axis=1, keepdims=True)
                p = jnp.exp(s - m).astype(BF16)
                lanes = slice(hsel * LANES, (hsel + 1) * LANES)
                vb = jnp.concatenate([rows_of(vv, (half_prev,), a_prev, lanes),
                                      rows_of(vv, (1,), a_cur, lanes)], axis=0).astype(BF16)
                pv = jnp.dot(p, vb, preferred_element_type=F32)
                val = jnp.where(lane == HEAD_DIM + 1, m, pv)
                for i in range(n_runs):
                    st[pi, hsel, i * d + res, pl.ds(a_cur, run), :] = val[i * run:(i + 1) * run]
            return carry

        lax.fori_loop(0, n_blocks, body, 0)

    def merge(g, carry):
        outs = []
        for hsel in range(2):
            parts = [st[pi, hsel, g] for pi in range(len(DILATIONS))]
            maxes = [t[:, HEAD_DIM + 1:HEAD_DIM + 2] for t in parts]
            top = functools.reduce(jnp.maximum, maxes)
            tot = sum(jnp.exp(mx - top) * t for mx, t in zip(maxes, parts))
            outs.append(tot / tot[:, HEAD_DIM:HEAD_DIM + 1])
        o_ref[0, 0, g] = jnp.where(head0, outs[0], pltpu.roll(outs[1], HEAD_DIM, 1)).astype(o_ref.dtype)
        return carry

    lax.fori_loop(0, ATTN_GROUP, merge, 0)


def _mix_kernel(co_ref, ao_ref, x_ref, mod_ref, wo_ref, unperm_ref, gffn_ref, wr_ref, br_ref,
                x1_ref, h2_ref, lg_ref):
    rows = x_ref.shape[1]
    ao = jnp.dot(unperm_ref[...], ao_ref[0, 0].reshape(rows, ATTN_WIDTH),
                 preferred_element_type=F32).astype(BF16)
    mixed = (jnp.dot(co_ref[0], wo_ref[0:CONV_CH, :], preferred_element_type=F32)
             + jnp.dot(ao, wo_ref[CONV_CH:, :], preferred_element_type=F32))
    x1 = x_ref[0] + mod_ref[0, 2:3, :] * mixed
    x1_ref[0] = x1
    h2 = _rms_modulate(x1, gffn_ref[...], mod_ref[0, 3:4, :], mod_ref[0, 4:5, :])
    for j in range(SLABS):
        h2_ref[pl.ds(j, rows, stride=SLABS), :] = h2[:, j * LANES:(j + 1) * LANES]
    lg_ref[...] = jnp.dot(h2, wr_ref[...], preferred_element_type=F32, precision=HIGHEST) + br_ref[...]


def _slab_rows(idx):
    return pl.ds(pl.multiple_of(idx * SLABS, SLABS), SLABS)


def _expert_kernel(meta_ref, idx_hbm, h2_hbm, wgu_ref, bgu_ref, wd_ref, bd_ref, y_hbm,
                   xbuf, ybuf, idx_a, idx_b, wgu_bf, wd_bf, gsem, ssem, isem):
    i = pl.program_id(0)
    n_used = meta_ref[0]
    expert = meta_ref[1 + i]
    prev_expert = meta_ref[jnp.maximum(i, 1)]
    cur = i % 2
    idx_bufs = (idx_a, idx_b)
    idx_row = 2 * MOE_ROWS

    def idx_copy(row, slot):
        start = pl.multiple_of(row * idx_row, idx_row)
        return pltpu.make_async_copy(idx_hbm.at[pl.ds(start, idx_row)], idx_bufs[slot], isem)

    def row_slab(r):
        return pl.ds(r * SLABS, SLABS) if isinstance(r, int) else _slab_rows(r)

    def gather_copy(idx_slot, buf_slot, r):
        return pltpu.make_async_copy(h2_hbm.at[_slab_rows(idx_bufs[idx_slot][r]), :],
                                     xbuf.at[buf_slot, row_slab(r), :], gsem)

    def scatter_copy(idx_slot, buf_slot, r):
        return pltpu.make_async_copy(ybuf.at[buf_slot, row_slab(r), :],
                                     y_hbm.at[_slab_rows(idx_bufs[idx_slot][MOE_ROWS + r]), :], ssem)

    def start_rows(idx_slot, buf_slot, gather, scatter, unrolled):
        def body(r, carry):
            if gather:
                gather_copy(idx_slot, buf_slot, r).start()
            if scatter:
                scatter_copy(idx_slot, buf_slot, r).start()
            return carry
        if unrolled:
            for r in range(MOE_ROWS):
                body(r, 0)
        else:
            lax.fori_loop(0, MOE_ROWS, body, 0, unroll=8)

    def wait_gathers():
        pltpu.make_async_copy(h2_hbm.at[pl.ds(0, MOE_ROWS * SLABS), :], xbuf.at[0], gsem).wait()

    def wait_scatters():
        pltpu.make_async_copy(ybuf.at[0], y_hbm.at[pl.ds(0, MOE_ROWS * SLABS), :], ssem).wait()

    @pl.when(i == 0)
    def _():
        first = idx_copy(0, 1)
        first.start()
        first.wait()
        start_rows(1, 0, True, False, False)
        wait_gathers()
        ybuf[1] = jnp.zeros((MOE_ROWS * SLABS, LANES), F32)
        second = idx_copy(1, 0)
        second.start()
        second.wait()

    @pl.when(jnp.logical_or(i == 0, expert != prev_expert))
    def _():
        wgu_bf[...] = wgu_ref[0].astype(BF16)
        wd_bf[...] = wd_ref[0].astype(BF16)

    for parity in (0, 1):
        @pl.when(jnp.logical_and(i < n_used, cur == parity))
        def _(parity=parity):
            idx_copy(i + 2, 1 - parity).start()
            start_rows(parity, 1 - parity, True, True, True)

        @pl.when(jnp.logical_and(i == n_used, cur == parity))
        def _(parity=parity):
            start_rows(parity, 1 - parity, False, True, False)
            wait_scatters()

    @pl.when(i < n_used)
    def _():
        x = jnp.concatenate([xbuf[cur, pl.ds(j, MOE_ROWS, stride=SLABS), :] for j in range(SLABS)],
                            axis=1).astype(BF16)
        y = jnp.broadcast_to(bd_ref[0], (MOE_ROWS, D_MODEL))
        for cidx in range(D_FF // MOE_COLS):
            lo = cidx * MOE_COLS
            gate = jnp.dot(x, wgu_bf[:, lo:lo + MOE_COLS], preferred_element_type=F32) + bgu_ref[0, :, lo:lo + MOE_COLS]
            up = (jnp.dot(x, wgu_bf[:, D_FF + lo:D_FF + lo + MOE_COLS], preferred_element_type=F32)
                  + bgu_ref[0, :, D_FF + lo:D_FF + lo + MOE_COLS])
            gate = jnp.minimum(gate, SWIGLU_LIMIT)
            up = jnp.clip(up, -SWIGLU_LIMIT, SWIGLU_LIMIT)
            act = (up + 1.0) * (gate * jax.nn.sigmoid(SWIGLU_ALPHA * gate))
            y = y + jnp.dot(act.astype(BF16), wd_bf[lo:lo + MOE_COLS, :], preferred_element_type=F32)
        for j in range(SLABS):
            ybuf[cur, pl.ds(j, MOE_ROWS, stride=SLABS), :] = y[:, j * LANES:(j + 1) * LANES]
        wait_gathers()
        wait_scatters()
        idx_copy(0, 0).wait()


def _combine_kernel(x1_ref, gate_ref, mod_ref, gfin_ref, y_ref, o_ref):
    gates = gate_ref[...]
    stride = TOP_K * SLABS
    cols = []
    for j in range(SLABS):
        acc = gates[:, 0:1] * y_ref[pl.ds(j, TOK_TILE, stride=stride), :]
        for k in range(1, TOP_K):
            acc = acc + gates[:, k:k + 1] * y_ref[pl.ds(k * SLABS + j, TOK_TILE, stride=stride), :]
        cols.append(acc)
    ffn = jnp.concatenate(cols, axis=1)
    x2 = x1_ref[...] + mod_ref[0, 5:6, :] * ffn
    ms = jnp.mean(x2 * x2, axis=-1, keepdims=True)
    o_ref[...] = x2 * lax.rsqrt(ms + NORM_EPS) * gfin_ref[...]


def _rotary_lane_tables(positions):
    half = ROT_DIM // 2
    inv_freq = ROPE_THETA ** (-jnp.arange(0, ROT_DIM, 2, dtype=F32) / ROT_DIM)
    ang = inv_freq.reshape((half,) + (1,) * positions.ndim) * positions.astype(F32)[None]
    cos, sin = jnp.moveaxis(jnp.cos(ang), 0, -1), jnp.moveaxis(jnp.sin(ang), 0, -1)
    ones = jnp.ones(cos.shape[:-1] + (HEAD_DIM - ROT_DIM,), F32)
    zeros = jnp.zeros_like(ones)
    zero_h = jnp.zeros_like(sin)
    rc = jnp.concatenate([cos, cos, ones], axis=-1)
    rsa = jnp.concatenate([-sin, zero_h, zeros], axis=-1)
    rsb = jnp.concatenate([zero_h, sin, zeros], axis=-1)
    reps = LANES // HEAD_DIM
    return tuple(jnp.tile(t, (1,) * positions.ndim + (reps,)) for t in (rc, rsa, rsb))


def _group_perm(rows):
    run = rows // ATTN_GROUP
    dst = np.arange(rows)
    perm = np.zeros((rows, rows), np.float32)
    perm[dst, ATTN_GROUP * (dst % run) + dst // run] = 1.0
    return jnp.asarray(perm, BF16)


def _band_bias():
    pos = np.arange(ATTN_BLK)
    tables = []
    for d in DILATIONS:
        n_runs = ATTN_GROUP // d
        run = ATTN_BLK // n_runs
        strided = (pos % run) * n_runs + pos // run
        dist = strided[:, None] + ATTN_BLK - np.concatenate([strided, strided + ATTN_BLK])[None, :]
        tables.append(np.where((dist >= 0) & (dist <= ATTN_BLK), 0.0, -np.inf))
    return jnp.asarray(np.stack(tables), F32)


def _routing(logits):
    n_tok = logits.shape[0]
    n_asg = n_tok * TOP_K
    i32 = jnp.int32
    top_val, top_idx = lax.top_k(logits, TOP_K)
    gates = jax.nn.softmax(top_val, axis=-1)
    key = top_idx.reshape(n_asg).astype(i32) * n_asg + jnp.arange(n_asg, dtype=i32)
    skey = lax.sort(key)
    asg_sorted = skey % n_asg
    experts = jnp.arange(N_EXPERTS, dtype=i32)
    counts = jnp.sum((top_idx.reshape(n_asg, 1) == experts[None, :]).astype(i32), axis=0)
    starts = jnp.cumsum(counts) - counts
    padded = (counts + MOE_ROWS - 1) // MOE_ROWS * MOE_ROWS
    pends = jnp.cumsum(padded)
    pstarts = pends - padded
    n_blocks = n_asg // MOE_ROWS + N_EXPERTS
    n_used = (pends[-1] // MOE_ROWS).astype(i32)
    blk = jnp.arange(n_blocks + 2, dtype=i32)
    blk_start = blk * MOE_ROWS
    clamped = jnp.minimum(blk_start, pends[-1] - MOE_ROWS)
    blk_expert = jnp.sum((pends[None, :] <= clamped[:, None]).astype(i32), axis=1)
    row = jnp.arange(MOE_ROWS, dtype=i32)[None, :]
    off = blk_start[:, None] + row - pstarts[blk_expert][:, None]
    valid = jnp.logical_and(off < counts[blk_expert][:, None], blk[:, None] < n_used)
    asg = asg_sorted[jnp.clip(starts[blk_expert][:, None] + off, 0, n_asg - 1)]
    tok = jnp.where(valid, asg // TOP_K, 0)
    spare = jnp.broadcast_to(n_asg + row, (2, MOE_ROWS))
    slot = jnp.concatenate([spare, jnp.where(valid, asg, n_asg + row)[:-2]], axis=0)
    idx = jnp.concatenate([tok, slot], axis=1).reshape(-1).astype(i32)
    meta = jnp.concatenate([n_used[None], blk_expert[:n_blocks + 1]])
    return gates, idx, meta, n_blocks


def kernel(x, c, positions, w_ada, b_ada, g_mix, w_in, conv_w, conv_b, conv_ln_g, conv_ln_b,
           w_out, g_ffn, w_router, b_router, w_gu, b_gu, w_down, b_down, g_final):
    bsz, seq, d = x.shape
    n_tok = bsz * seq
    depth = w_ada.shape[0]
    assert depth == 1, "the final RMSNorm is fused into the layer's combine kernel"
    n_chunks = seq // ATTN_CHUNK
    grp_rows = ATTN_CHUNK // ATTN_GROUP
    tile_run = PROJ_ROWS // ATTN_GROUP
    tiles_per_chunk = ATTN_CHUNK // PROJ_ROWS
    grouped_pos = positions.reshape(bsz, n_chunks, grp_rows, ATTN_GROUP).swapaxes(2, 3)
    rc, rsa, rsb = _rotary_lane_tables(grouped_pos)
    perm = _group_perm(PROJ_ROWS)
    s_tiles = seq // PROJ_ROWS
    arb2 = pltpu.CompilerParams(dimension_semantics=("arbitrary", "arbitrary"), vmem_limit_bytes=VMEM_LIMIT)
    arb1 = pltpu.CompilerParams(dimension_semantics=("arbitrary",), vmem_limit_bytes=VMEM_LIMIT)

    for l in range(depth):
        n_mod = 6 * d
        mod = pl.pallas_call(
            _ada_kernel,
            grid=(n_mod // d,),
            in_specs=[pl.BlockSpec((bsz, d), lambda j: (0, 0)),
                      pl.BlockSpec((d, d), lambda j: (0, j)),
                      pl.BlockSpec((1, d), lambda j: (0, j))],
            out_specs=pl.BlockSpec((bsz, d), lambda j: (0, j)),
            out_shape=jax.ShapeDtypeStruct((bsz, n_mod), F32),
            name="ada",
        )(c, w_ada[l], b_ada[l].reshape(1, n_mod))
        mod = mod.reshape(bsz, 6, d)

        row_spec = lambda width: pl.BlockSpec((1, PROJ_ROWS, width), lambda b, s: (b, s, 0))
        grp_spec = lambda width: pl.BlockSpec(
            (1, 1, ATTN_GROUP, tile_run, width), lambda b, s: (b, s // tiles_per_chunk, 0, s % tiles_per_chunk, 0))
        grp_shape = lambda width, dtype: jax.ShapeDtypeStruct((bsz, n_chunks, ATTN_GROUP, grp_rows, width), dtype)
        full2 = lambda shape: pl.BlockSpec(shape, lambda b, s: (0, 0))
        mod_spec = pl.BlockSpec((1, 6, d), lambda b, s: (b, 0, 0))
        conv_out, q, k, v = pl.pallas_call(
            _proj_kernel,
            grid=(bsz, s_tiles),
            in_specs=[row_spec(d), mod_spec, full2((1, d)), full2((d, IN_COLS)), full2((PROJ_ROWS, PROJ_ROWS)),
                      full2((CONV_WIDTH, CONV_CH)), full2((1, CONV_CH)), full2((1, CONV_CH)), full2((1, CONV_CH)),
                      grp_spec(LANES), grp_spec(LANES), grp_spec(LANES)],
            out_specs=[row_spec(CONV_CH), grp_spec(ATTN_WIDTH), grp_spec(ATTN_WIDTH), grp_spec(2 * ATTN_WIDTH)],
            out_shape=[jax.ShapeDtypeStruct((bsz, seq, CONV_CH), BF16),
                       grp_shape(ATTN_WIDTH, F32), grp_shape(ATTN_WIDTH, F32), grp_shape(2 * ATTN_WIDTH, F32)],
            scratch_shapes=[pltpu.VMEM((PROJ_ROWS + CONV_HALO, CONV_CH), F32)],
            compiler_params=arb2,
            name="proj",
        )(x, mod, g_mix[l].reshape(1, d), w_in[l].astype(BF16), perm, conv_w[l].reshape(CONV_WIDTH, CONV_CH),
          conv_b[l].reshape(1, CONV_CH), conv_ln_g[l].reshape(1, CONV_CH), conv_ln_b[l].reshape(1, CONV_CH),
          rc, rsa, rsb)

        def chunk_spec(width, back):
            return pl.BlockSpec((1, 1, ATTN_GROUP, grp_rows, width),
                                lambda b, p, ch: (b, jnp.maximum(ch - back, 0), 0, 0, p))
        attn_out = pl.pallas_call(
            _attn_kernel,
            grid=(bsz, ATTN_WIDTH // LANES, n_chunks),
            in_specs=[pl.BlockSpec((len(DILATIONS), ATTN_BLK, 2 * ATTN_BLK), lambda b, p, ch: (0, 0, 0)),
                      chunk_spec(LANES, 0), chunk_spec(LANES, 1), chunk_spec(LANES, 0),
                      chunk_spec(2 * LANES, 1), chunk_spec(2 * LANES, 0)],
            out_specs=chunk_spec(LANES, 0),
            out_shape=grp_shape(ATTN_WIDTH, BF16),
            scratch_shapes=[pltpu.VMEM((2, ATTN_GROUP, grp_rows, LANES), F32),
                            pltpu.VMEM((2, ATTN_GROUP, grp_rows, 2 * LANES), F32),
                            pltpu.VMEM((len(DILATIONS), 2, ATTN_GROUP, grp_rows, LANES), F32)],
            compiler_params=pltpu.CompilerParams(dimension_semantics=("arbitrary",) * 3,
                                                 vmem_limit_bytes=VMEM_LIMIT),
            name="attn",
        )(_band_bias(), q, k, k, v, v)

        x1, h2, logits = pl.pallas_call(
            _mix_kernel,
            grid=(bsz, s_tiles),
            in_specs=[row_spec(CONV_CH), grp_spec(ATTN_WIDTH), row_spec(d), mod_spec, full2((d, d)),
                      full2((PROJ_ROWS, PROJ_ROWS)), full2((1, d)), full2((d, N_EXPERTS)), full2((1, N_EXPERTS))],
            out_specs=[row_spec(d),
                       pl.BlockSpec((PROJ_ROWS * SLABS, LANES), lambda b, s: (b * s_tiles + s, 0)),
                       pl.BlockSpec((PROJ_ROWS, N_EXPERTS), lambda b, s: (b * s_tiles + s, 0))],
            out_shape=[jax.ShapeDtypeStruct((bsz, seq, d), F32),
                       jax.ShapeDtypeStruct((n_tok * SLABS, LANES), F32),
                       jax.ShapeDtypeStruct((n_tok, N_EXPERTS), F32)],
            compiler_params=arb2,
            name="mix",
        )(conv_out, attn_out, x, mod, w_out[l].astype(BF16), perm.T, g_ffn[l].reshape(1, d),
          w_router[l], b_router[l].reshape(1, N_EXPERTS))

        gates, idx, meta, n_blocks = _routing(logits)
        n_slots = n_tok * TOP_K + MOE_ROWS
        n_tiles = n_tok // TOK_TILE

        def expert_block(i, meta):
            return (meta[1 + i], 0, 0)

        any_spec = pl.BlockSpec(memory_space=pl.ANY)
        y = pl.pallas_call(
            _expert_kernel,
            grid_spec=pltpu.PrefetchScalarGridSpec(
                num_scalar_prefetch=1,
                grid=(n_blocks + 1,),
                in_specs=[any_spec, any_spec,
                          pl.BlockSpec((1, d, 2 * D_FF), expert_block),
                          pl.BlockSpec((1, 1, 2 * D_FF), expert_block),
                          pl.BlockSpec((1, D_FF, d), expert_block),
                          pl.BlockSpec((1, 1, d), expert_block)],
                out_specs=any_spec,
                scratch_shapes=[pltpu.VMEM((2, MOE_ROWS * SLABS, LANES), F32),
                                pltpu.VMEM((2, MOE_ROWS * SLABS, LANES), F32),
                                pltpu.SMEM((2 * MOE_ROWS,), jnp.int32), pltpu.SMEM((2 * MOE_ROWS,), jnp.int32),
                                pltpu.VMEM((d, 2 * D_FF), BF16), pltpu.VMEM((D_FF, d), BF16),
                                pltpu.SemaphoreType.DMA(()), pltpu.SemaphoreType.DMA(()),
                                pltpu.SemaphoreType.DMA(())]),
            out_shape=jax.ShapeDtypeStruct((n_slots * SLABS, LANES), F32),
            compiler_params=arb1,
            name="experts",
        )(meta, idx, h2, w_gu[l], b_gu[l].reshape(N_EXPERTS, 1, 2 * D_FF), w_down[l],
          b_down[l].reshape(N_EXPERTS, 1, d))

        out = pl.pallas_call(
            _combine_kernel,
            grid=(n_tiles,),
            in_specs=[pl.BlockSpec((TOK_TILE, d), lambda i: (i, 0)),
                      pl.BlockSpec((TOK_TILE, TOP_K), lambda i: (i, 0)),
                      pl.BlockSpec((1, 6, d), lambda i: (i // (seq // TOK_TILE), 0, 0)),
                      pl.BlockSpec((1, d), lambda i: (0, 0)),
                      pl.BlockSpec((TOK_TILE * TOP_K * SLABS, LANES), lambda i: (i, 0))],
            out_specs=pl.BlockSpec((TOK_TILE, d), lambda i: (i, 0)),
            out_shape=jax.ShapeDtypeStruct((n_tok, d), F32),
            compiler_params=arb1,
            name="combine",
        )(x1.reshape(n_tok, d), gates, mod, g_final.reshape(1, d), y)
        x = out.reshape(bsz, seq, d)
    return x
```

```python
import functools

import jax
import numpy as np
import jax.numpy as jnp
from jax import lax
from jax.experimental import pallas as pl
from jax.experimental.pallas import tpu as pltpu

F32 = jnp.float32
BF16 = jnp.bfloat16

D_MODEL = 1024
N_HEADS = 8
HEAD_DIM = 64
ATTN_WIDTH = N_HEADS * HEAD_DIM
CONV_CH = D_MODEL - ATTN_WIDTH
IN_COLS = 2 * CONV_CH + 3 * ATTN_WIDTH
CONV_WIDTH = 31
ROT_DIM = HEAD_DIM // 4
ROPE_THETA = 500000.0
DILATIONS = (1, 4, 16)
ATTN_BLK = 128
ATTN_GROUP = 16
N_EXPERTS = 32
TOP_K = 4
D_FF = D_MODEL
SWIGLU_LIMIT = 7.0
SWIGLU_ALPHA = 1.702
NORM_EPS = 1e-6

LANES = 128
SUBLANES = 8
SLABS = D_MODEL // LANES
VMEM_LIMIT = 56 * 1024 * 1024

PROJ_ROWS = 512
CONV_HALO = 32
CONV_ROWS = 64
ATTN_CHUNK = 2048
MOE_ROWS = 512
MOE_COLS = 256
TOK_TILE = 256


def _ada_kernel(c_ref, w_ref, b_ref, o_ref):
    c = c_ref[...]
    c_act = c * jax.nn.sigmoid(c)
    o_ref[...] = jnp.dot(c_act.astype(BF16), w_ref[...].astype(BF16), preferred_element_type=F32) + b_ref[...]


def _rms_modulate(x, gain, shift, scale):
    ms = jnp.mean(x * x, axis=-1, keepdims=True)
    y = x * lax.rsqrt(ms + NORM_EPS) * gain
    return y * (1.0 + scale) + shift


def _proj_kernel(x_ref, mod_ref, gmix_ref, win_ref, perm_ref, cw_ref, cb_ref, lng_ref, lnb_ref,
                 rc_ref, rsa_ref, rsb_ref, co_ref, q_ref, k_ref, v_ref, ubuf):
    rows = x_ref.shape[1]
    hb = _rms_modulate(x_ref[0], gmix_ref[...], mod_ref[0, 0:1, :], mod_ref[0, 1:2, :]).astype(BF16)

    ag = jnp.dot(hb, win_ref[:, :2 * CONV_CH], preferred_element_type=F32)
    u = ag[:, :CONV_CH] * jax.nn.sigmoid(ag[:, CONV_CH:])

    @pl.when(pl.program_id(1) == 0)
    def _():
        ubuf[0:CONV_HALO, :] = jnp.zeros((CONV_HALO, CONV_CH), F32)

    ubuf[CONV_HALO:, :] = u
    first_tap = CONV_HALO - (CONV_WIDTH - 1)
    for ch in range(rows // CONV_ROWS):
        row0 = ch * CONV_ROWS
        acc = jnp.broadcast_to(cb_ref[...], (CONV_ROWS, CONV_CH))
        for shift in range(SUBLANES):
            taps = [j for j in range(CONV_WIDTH) if (first_tap + j) % SUBLANES == shift]
            span = CONV_ROWS + (SUBLANES if shift else 0)
            part = None
            for j in taps:
                base = row0 + first_tap + j - shift
                term = cw_ref[j:j + 1, :] * ubuf[base:base + span, :]
                part = term if part is None else part + term
            acc = acc + part[shift:shift + CONV_ROWS]
        mu = jnp.mean(acc, axis=-1, keepdims=True)
        cen = acc - mu
        var = jnp.mean(cen * cen, axis=-1, keepdims=True)
        yn = cen * lax.rsqrt(var + NORM_EPS) * lng_ref[...] + lnb_ref[...]
        co_ref[0, ch * CONV_ROWS:(ch + 1) * CONV_ROWS, :] = (yn * jax.nn.sigmoid(yn)).astype(co_ref.dtype)
    ubuf[0:CONV_HALO, :] = ubuf[rows:rows + CONV_HALO, :]

    hp = jnp.dot(perm_ref[...], hb, preferred_element_type=F32).astype(BF16)
    qkv = jnp.dot(hp, win_ref[:, 2 * CONV_CH:], preferred_element_type=F32)
    run = rows // ATTN_GROUP
    tile = (ATTN_GROUP, run, LANES)
    rc, rsa, rsb = (t[0, 0].reshape(rows, LANES) for t in (rc_ref, rsa_ref, rsb_ref))
    lane = lax.broadcasted_iota(jnp.int32, (rows, LANES), 1)
    ones_col = jnp.where(lane == HEAD_DIM, 1.0, 0.0)
    half = ROT_DIM // 2
    for blk in range(ATTN_WIDTH // LANES):
        lo, hi = blk * LANES, (blk + 1) * LANES
        for src0, dst_ref, scale in ((0, q_ref, HEAD_DIM ** -0.5), (ATTN_WIDTH, k_ref, 1.0)):
            t = qkv[:, src0 + lo:src0 + hi]
            rot = t * rc + pltpu.roll(t, LANES - half, 1) * rsa + pltpu.roll(t, half, 1) * rsb
            dst_ref[0, 0, :, :, lo:hi] = (rot * scale).reshape(tile)
        vb = qkv[:, 2 * ATTN_WIDTH + lo:2 * ATTN_WIDTH + hi]
        for hsel, vals in enumerate((vb, pltpu.roll(vb, HEAD_DIM, 1))):
            vh = jnp.where(lane < HEAD_DIM, vals, ones_col)
            v_ref[0, 0, :, :, (2 * blk + hsel) * LANES:(2 * blk + hsel + 1) * LANES] = vh.reshape(tile)


def _attn_kernel(bias_ref, q_ref, kp_ref, kc_ref, vp_ref, vc_ref, o_ref, st, st_max):
    chunk = pl.program_id(2)
    lane = lax.broadcasted_iota(jnp.int32, (ATTN_BLK, LANES), 1)
    head0 = lane < HEAD_DIM
    kcol = lax.broadcasted_iota(jnp.int32, (ATTN_BLK, 2 * ATTN_BLK), 1)
    neg_inf = jnp.float32(-jnp.inf)
    n_blocks = ATTN_CHUNK // ATTN_BLK
    all_lanes = slice(None)

    for pi, d in enumerate(DILATIONS):
        n_runs = ATTN_GROUP // d
        run = ATTN_BLK // n_runs
        n_sub = n_blocks // d
        bias = bias_ref[pi]
        bias_first = jnp.where(jnp.logical_or(kcol >= ATTN_BLK, chunk > 0), bias, neg_inf)
        for idx in range(n_blocks):
            u, res = idx // d, idx % d
            a_cur = u * run
            prev_k, prev_v, a_prev = (kc_ref, vc_ref, a_cur - run) if u > 0 else (kp_ref, vp_ref, (n_sub - 1) * run)

            def rows_of(ref, a0, lanes):
                parts = [ref[0, 0, i * d + res, a0:a0 + run, lanes] for i in range(n_runs)]
                return parts[0] if n_runs == 1 else jnp.concatenate(parts, axis=0)

            qb = rows_of(q_ref, a_cur, all_lanes)
            kb = jnp.concatenate([rows_of(prev_k, a_prev, all_lanes),
                                  rows_of(kc_ref, a_cur, all_lanes)], axis=0).astype(BF16)
            q2 = jnp.concatenate([jnp.where(head0, qb, 0.0), jnp.where(head0, 0.0, qb)], axis=0).astype(BF16)
            s2 = lax.dot_general(q2, kb, (((1,), (1,)), ((), ())), preferred_element_type=F32)
            for hsel in range(2):
                s = s2[hsel * ATTN_BLK:(hsel + 1) * ATTN_BLK] + (bias if u > 0 else bias_first)
                m = jnp.max(s, axis=1, keepdims=True)
                p = jnp.exp(s - m).astype(BF16)
                lanes = slice(hsel * LANES, (hsel + 1) * LANES)
                vb = jnp.concatenate([rows_of(prev_v, a_prev, lanes),
                                      rows_of(vc_ref, a_cur, lanes)], axis=0).astype(BF16)
                pv = jnp.dot(p, vb, preferred_element_type=F32)
                mfull = jnp.broadcast_to(m, (ATTN_BLK, LANES))
                for i in range(n_runs):
                    st[pi, hsel, i * d + res, a_cur:a_cur + run, :] = pv[i * run:(i + 1) * run]
                    st_max[pi, hsel, i * d + res, a_cur:a_cur + run, :] = mfull[i * run:(i + 1) * run]

    def merge(g, carry):
        outs = []
        for hsel in range(2):
            maxes = [st_max[pi, hsel, g] for pi in range(len(DILATIONS))]
            top = functools.reduce(jnp.maximum, maxes)
            tot = sum(jnp.exp(mx - top) * st[pi, hsel, g] for pi, mx in enumerate(maxes))
            outs.append(tot / tot[:, HEAD_DIM:HEAD_DIM + 1])
        o_ref[0, 0, g] = jnp.where(head0, outs[0], pltpu.roll(outs[1], HEAD_DIM, 1)).astype(o_ref.dtype)
        return carry

    lax.fori_loop(0, ATTN_GROUP, merge, 0, unroll=2)


def _mix_kernel(co_ref, ao_ref, x_ref, mod_ref, wo_ref, unperm_ref, gffn_ref, wr_ref, br_ref,
                x1_ref, h2_ref, lg_ref):
    rows = x_ref.shape[1]
    ao = jnp.dot(unperm_ref[...], ao_ref[0, 0].reshape(rows, ATTN_WIDTH),
                 preferred_element_type=F32).astype(BF16)
    mixed = (jnp.dot(co_ref[0], wo_ref[0:CONV_CH, :], preferred_element_type=F32)
             + jnp.dot(ao, wo_ref[CONV_CH:, :], preferred_element_type=F32))
    x1 = x_ref[0] + mod_ref[0, 2:3, :] * mixed
    x1_ref[0] = x1
    h2 = _rms_modulate(x1, gffn_ref[...], mod_ref[0, 3:4, :], mod_ref[0, 4:5, :])
    for j in range(SLABS):
        h2_ref[pl.ds(j, rows, stride=SLABS), :] = h2[:, j * LANES:(j + 1) * LANES]
    lg_ref[...] = jnp.dot(h2.astype(BF16), wr_ref[...], preferred_element_type=F32) + br_ref[...]


def _slab_rows(idx):
    return pl.ds(pl.multiple_of(idx * SLABS, SLABS), SLABS)


def _expert_kernel(meta_ref, idx_hbm, h2_hbm, wgu_ref, bgu_ref, wd_ref, bd_ref, y_hbm,
                   xbuf, ybuf, idx_a, idx_b, wgu_bf, wd_bf, gsem, ssem, isem):
    i = pl.program_id(0)
    n_used = meta_ref[0]
    expert = meta_ref[1 + i]
    prev_expert = meta_ref[jnp.maximum(i, 1)]
    cur = i % 2
    idx_bufs = (idx_a, idx_b)
    idx_row = 2 * MOE_ROWS

    def idx_copy(row, slot):
        start = pl.multiple_of(row * idx_row, idx_row)
        return pltpu.make_async_copy(idx_hbm.at[pl.ds(start, idx_row)], idx_bufs[slot], isem)

    def row_slab(r):
        return pl.ds(r * SLABS, SLABS) if isinstance(r, int) else _slab_rows(r)

    def gather_copy(idx_slot, buf_slot, r):
        return pltpu.make_async_copy(h2_hbm.at[_slab_rows(idx_bufs[idx_slot][r]), :],
                                     xbuf.at[buf_slot, row_slab(r), :], gsem)

    def scatter_copy(idx_slot, buf_slot, r):
        return pltpu.make_async_copy(ybuf.at[buf_slot, row_slab(r), :],
                                     y_hbm.at[_slab_rows(idx_bufs[idx_slot][MOE_ROWS + r]), :], ssem)

    def start_rows(idx_slot, buf_slot, gather, scatter, unrolled):
        def body(r, carry):
            if gather:
                gather_copy(idx_slot, buf_slot, r).start()
            if scatter:
                scatter_copy(idx_slot, buf_slot, r).start()
            return carry
        if unrolled:
            for r in range(MOE_ROWS):
                body(r, 0)
        else:
            lax.fori_loop(0, MOE_ROWS, body, 0, unroll=8)

    def wait_gathers():
        pltpu.make_async_copy(h2_hbm.at[pl.ds(0, MOE_ROWS * SLABS), :], xbuf.at[0], gsem).wait()

    def wait_scatters():
        pltpu.make_async_copy(ybuf.at[0], y_hbm.at[pl.ds(0, MOE_ROWS * SLABS), :], ssem).wait()

    @pl.when(i == 0)
    def _():
        first = idx_copy(0, 1)
        first.start()
        first.wait()
        start_rows(1, 0, True, False, False)
        wait_gathers()
        ybuf[1] = jnp.zeros((MOE_ROWS * SLABS, LANES), F32)
        second = idx_copy(1, 0)
        second.start()
        second.wait()

    @pl.when(jnp.logical_or(i == 0, expert != prev_expert))
    def _():
        wgu_bf[...] = wgu_ref[0].astype(BF16)
        wd_bf[...] = wd_ref[0].astype(BF16)

    for parity in (0, 1):
        @pl.when(jnp.logical_and(i < n_used, cur == parity))
        def _(parity=parity):
            idx_copy(i + 2, 1 - parity).start()
            start_rows(parity, 1 - parity, True, True, True)

        @pl.when(jnp.logical_and(i == n_used, cur == parity))
        def _(parity=parity):
            start_rows(parity, 1 - parity, False, True, False)
            wait_scatters()

    @pl.when(i < n_used)
    def _():
        x = jnp.concatenate([xbuf[cur, pl.ds(j, MOE_ROWS, stride=SLABS), :] for j in range(SLABS)],
                            axis=1).astype(BF16)
        y = jnp.broadcast_to(bd_ref[0], (MOE_ROWS, D_MODEL))
        for cidx in range(D_FF // MOE_COLS):
            lo = cidx * MOE_COLS
            gate = jnp.dot(x, wgu_bf[:, lo:lo + MOE_COLS], preferred_element_type=F32) + bgu_ref[0, :, lo:lo + MOE_COLS]
            up = (jnp.dot(x, wgu_bf[:, D_FF + lo:D_FF + lo + MOE_COLS], preferred_element_type=F32)
                  + bgu_ref[0, :, D_FF + lo:D_FF + lo + MOE_COLS])
            gate = jnp.minimum(gate, SWIGLU_LIMIT)
            up = jnp.clip(up, -SWIGLU_LIMIT, SWIGLU_LIMIT)
            act = (up + 1.0) * (gate * jax.nn.sigmoid(SWIGLU_ALPHA * gate))
            y = y + jnp.dot(act.astype(BF16), wd_bf[lo:lo + MOE_COLS, :], preferred_element_type=F32)
        for j in range(SLABS):
            ybuf[cur, pl.ds(j, MOE_ROWS, stride=SLABS), :] = y[:, j * LANES:(j + 1) * LANES]
        wait_gathers()
        wait_scatters()
        idx_copy(0, 0).wait()


def _combine_kernel(x1_ref, gate_ref, mod_ref, gfin_ref, y_ref, o_ref):
    gates = gate_ref[...]
    stride = TOP_K * SLABS
    cols = []
    for j in range(SLABS):
        acc = gates[:, 0:1] * y_ref[pl.ds(j, TOK_TILE, stride=stride), :]
        for k in range(1, TOP_K):
            acc = acc + gates[:, k:k + 1] * y_ref[pl.ds(k * SLABS + j, TOK_TILE, stride=stride), :]
        cols.append(acc)
    ffn = jnp.concatenate(cols, axis=1)
    x2 = x1_ref[...] + mod_ref[0, 5:6, :] * ffn
    ms = jnp.mean(x2 * x2, axis=-1, keepdims=True)
    o_ref[...] = x2 * lax.rsqrt(ms + NORM_EPS) * gfin_ref[...]


def _rotary_lane_tables(positions):
    half = ROT_DIM // 2
    inv_freq = ROPE_THETA ** (-jnp.arange(0, ROT_DIM, 2, dtype=F32) / ROT_DIM)
    ang = inv_freq.reshape((half,) + (1,) * positions.ndim) * positions.astype(F32)[None]
    cos, sin = jnp.moveaxis(jnp.cos(ang), 0, -1), jnp.moveaxis(jnp.sin(ang), 0, -1)
    ones = jnp.ones(cos.shape[:-1] + (HEAD_DIM - ROT_DIM,), F32)
    zeros = jnp.zeros_like(ones)
    zero_h = jnp.zeros_like(sin)
    rc = jnp.concatenate([cos, cos, ones], axis=-1)
    rsa = jnp.concatenate([-sin, zero_h, zeros], axis=-1)
    rsb = jnp.concatenate([zero_h, sin, zeros], axis=-1)
    reps = LANES // HEAD_DIM
    return tuple(jnp.tile(t, (1,) * positions.ndim + (reps,)) for t in (rc, rsa, rsb))


def _group_perm(rows):
    run = rows // ATTN_GROUP
    dst = np.arange(rows)
    perm = np.zeros((rows, rows), np.float32)
    perm[dst, ATTN_GROUP * (dst % run) + dst // run] = 1.0
    return jnp.asarray(perm, BF16)


def _band_bias():
    pos = np.arange(ATTN_BLK)
    tables = []
    for d in DILATIONS:
        n_runs = ATTN_GROUP // d
        run = ATTN_BLK // n_runs
        strided = (pos % run) * n_runs + pos // run
        dist = strided[:, None] + ATTN_BLK - np.concatenate([strided, strided + ATTN_BLK])[None, :]
        tables.append(np.where((dist >= 0) & (dist <= ATTN_BLK), 0.0, -np.inf))
    return jnp.asarray(np.stack(tables), F32)


def _attention(q, k, v):
    bsz, n_chunks, _, grp_rows, _ = q.shape

    def chunk_spec(width, back):
        return pl.BlockSpec((1, 1, ATTN_GROUP, grp_rows, width),
                            lambda b, p, ch: (b, jnp.maximum(ch - back, 0), 0, 0, p))
    return pl.pallas_call(
        _attn_kernel,
        grid=(bsz, ATTN_WIDTH // LANES, n_chunks),
        in_specs=[pl.BlockSpec((len(DILATIONS), ATTN_BLK, 2 * ATTN_BLK), lambda b, p, ch: (0, 0, 0)),
                  chunk_spec(LANES, 0), chunk_spec(LANES, 1), chunk_spec(LANES, 0),
                  chunk_spec(2 * LANES, 1), chunk_spec(2 * LANES, 0)],
        out_specs=chunk_spec(LANES, 0),
        out_shape=jax.ShapeDtypeStruct(q.shape, BF16),
        scratch_shapes=[pltpu.VMEM((len(DILATIONS), 2, ATTN_GROUP, grp_rows, LANES), F32),
                        pltpu.VMEM((len(DILATIONS), 2, ATTN_GROUP, grp_rows, LANES), F32)],
        compiler_params=pltpu.CompilerParams(dimension_semantics=("arbitrary",) * 3,
                                             vmem_limit_bytes=VMEM_LIMIT),
        name="attn",
    )(_band_bias(), q, k, k, v, v)


def _routing(logits):
    n_tok = logits.shape[0]
    n_asg = n_tok * TOP_K
    i32 = jnp.int32
    top_val, top_idx = lax.top_k(logits, TOP_K)
    gates = jax.nn.softmax(top_val, axis=-1)
    key = top_idx.reshape(n_asg).astype(i32) * n_asg + jnp.arange(n_asg, dtype=i32)
    skey = lax.sort(key)
    asg_sorted = skey % n_asg
    experts = jnp.arange(N_EXPERTS, dtype=i32)
    counts = jnp.sum((top_idx.reshape(n_asg, 1) == experts[None, :]).astype(i32), axis=0)
    starts = jnp.cumsum(counts) - counts
    padded = (counts + MOE_ROWS - 1) // MOE_ROWS * MOE_ROWS
    pends = jnp.cumsum(padded)
    pstarts = pends - padded
    n_blocks = n_asg // MOE_ROWS + N_EXPERTS
    n_used = (pends[-1] // MOE_ROWS).astype(i32)
    blk = jnp.arange(n_blocks + 2, dtype=i32)
    blk_start = blk * MOE_ROWS
    clamped = jnp.minimum(blk_start, pends[-1] - MOE_ROWS)
    blk_expert = jnp.sum((pends[None, :] <= clamped[:, None]).astype(i32), axis=1)
    row = jnp.arange(MOE_ROWS, dtype=i32)[None, :]
    off = blk_start[:, None] + row - pstarts[blk_expert][:, None]
    valid = jnp.logical_and(off < counts[blk_expert][:, None], blk[:, None] < n_used)
    asg = asg_sorted[jnp.clip(starts[blk_expert][:, None] + off, 0, n_asg - 1)]
    tok = jnp.where(valid, asg // TOP_K, 0)
    spare = jnp.broadcast_to(n_asg + row, (2, MOE_ROWS))
    slot = jnp.concatenate([spare, jnp.where(valid, asg, n_asg + row)[:-2]], axis=0)
    idx = jnp.concatenate([tok, slot], axis=1).reshape(-1).astype(i32)
    meta = jnp.concatenate([n_used[None], blk_expert[:n_blocks + 1]])
    return gates, idx, meta, n_blocks


def kernel(x, c, positions, w_ada, b_ada, g_mix, w_in, conv_w, conv_b, conv_ln_g, conv_ln_b,
           w_out, g_ffn, w_router, b_router, w_gu, b_gu, w_down, b_down, g_final):
    bsz, seq, d = x.shape
    n_tok = bsz * seq
    depth = w_ada.shape[0]
    assert depth == 1, "the final RMSNorm is fused into the layer's combine kernel"
    n_chunks = seq // ATTN_CHUNK
    grp_rows = ATTN_CHUNK // ATTN_GROUP
    tile_run = PROJ_ROWS // ATTN_GROUP
    tiles_per_chunk = ATTN_CHUNK // PROJ_ROWS
    grouped_pos = positions.reshape(bsz, n_chunks, grp_rows, ATTN_GROUP).swapaxes(2, 3)
    rc, rsa, rsb = _rotary_lane_tables(grouped_pos)
    perm = _group_perm(PROJ_ROWS)
    s_tiles = seq // PROJ_ROWS
    arb2 = pltpu.CompilerParams(dimension_semantics=("arbitrary", "arbitrary"), vmem_limit_bytes=VMEM_LIMIT)
    arb1 = pltpu.CompilerParams(dimension_semantics=("arbitrary",), vmem_limit_bytes=VMEM_LIMIT)

    for l in range(depth):
        n_mod = 6 * d
        mod = pl.pallas_call(
            _ada_kernel,
            grid=(n_mod // d,),
            in_specs=[pl.BlockSpec((bsz, d), lambda j: (0, 0)),
                      pl.BlockSpec((d, d), lambda j: (0, j)),
                      pl.BlockSpec((1, d), lambda j: (0, j))],
            out_specs=pl.BlockSpec((bsz, d), lambda j: (0, j)),
            out_shape=jax.ShapeDtypeStruct((bsz, n_mod), F32),
            name="ada",
        )(c, w_ada[l], b_ada[l].reshape(1, n_mod))
        mod = mod.reshape(bsz, 6, d)

        row_spec = lambda width: pl.BlockSpec((1, PROJ_ROWS, width), lambda b, s: (b, s, 0))
        grp_spec = lambda width: pl.BlockSpec(
            (1, 1, ATTN_GROUP, tile_run, width), lambda b, s: (b, s // tiles_per_chunk, 0, s % tiles_per_chunk, 0))
        grp_shape = lambda width, dtype: jax.ShapeDtypeStruct((bsz, n_chunks, ATTN_GROUP, grp_rows, width), dtype)
        full2 = lambda shape: pl.BlockSpec(shape, lambda b, s: (0, 0))
        mod_spec = pl.BlockSpec((1, 6, d), lambda b, s: (b, 0, 0))
        conv_out, q, k, v = pl.pallas_call(
            _proj_kernel,
            grid=(bsz, s_tiles),
            in_specs=[row_spec(d), mod_spec, full2((1, d)), full2((d, IN_COLS)), full2((PROJ_ROWS, PROJ_ROWS)),
                      full2((CONV_WIDTH, CONV_CH)), full2((1, CONV_CH)), full2((1, CONV_CH)), full2((1, CONV_CH)),
                      grp_spec(LANES), grp_spec(LANES), grp_spec(LANES)],
            out_specs=[row_spec(CONV_CH), grp_spec(ATTN_WIDTH), grp_spec(ATTN_WIDTH), grp_spec(2 * ATTN_WIDTH)],
            out_shape=[jax.ShapeDtypeStruct((bsz, seq, CONV_CH), BF16),
                       grp_shape(ATTN_WIDTH, F32), grp_shape(ATTN_WIDTH, F32), grp_shape(2 * ATTN_WIDTH, F32)],
            scratch_shapes=[pltpu.VMEM((PROJ_ROWS + CONV_HALO, CONV_CH), F32)],
            compiler_params=arb2,
            name="proj",
        )(x, mod, g_mix[l].reshape(1, d), w_in[l].astype(BF16), perm, conv_w[l].reshape(CONV_WIDTH, CONV_CH),
          conv_b[l].reshape(1, CONV_CH), conv_ln_g[l].reshape(1, CONV_CH), conv_ln_b[l].reshape(1, CONV_CH),
          rc, rsa, rsb)

        attn_out = _attention(q, k, v)

        x1, h2, logits = pl.pallas_call(
            _mix_kernel,
            grid=(bsz, s_tiles),
            in_specs=[row_spec(CONV_CH), grp_spec(ATTN_WIDTH), row_spec(d), mod_spec, full2((d, d)),
                      full2((PROJ_ROWS, PROJ_ROWS)), full2((1, d)), full2((d, N_EXPERTS)), full2((1, N_EXPERTS))],
            out_specs=[row_spec(d),
                       pl.BlockSpec((PROJ_ROWS * SLABS, LANES), lambda b, s: (b * s_tiles + s, 0)),
                       pl.BlockSpec((PROJ_ROWS, N_EXPERTS), lambda b, s: (b * s_tiles + s, 0))],
            out_shape=[jax.ShapeDtypeStruct((bsz, seq, d), F32),
                       jax.ShapeDtypeStruct((n_tok * SLABS, LANES), F32),
                       jax.ShapeDtypeStruct((n_tok, N_EXPERTS), F32)],
            compiler_params=arb2,
            name="mix",
        )(conv_out, attn_out, x, mod, w_out[l].astype(BF16), perm.T, g_ffn[l].reshape(1, d),
          w_router[l].astype(BF16), b_router[l].reshape(1, N_EXPERTS))

        gates, idx, meta, n_blocks = _routing(logits)
        n_slots = n_tok * TOP_K + MOE_ROWS
        n_tiles = n_tok // TOK_TILE

        def expert_block(i, meta):
            return (meta[1 + i], 0, 0)

        any_spec = pl.BlockSpec(memory_space=pl.ANY)
        y = pl.pallas_call(
            _expert_kernel,
            grid_spec=pltpu.PrefetchScalarGridSpec(
                num_scalar_prefetch=1,
                grid=(n_blocks + 1,),
                in_specs=[any_spec, any_spec,
                          pl.BlockSpec((1, d, 2 * D_FF), expert_block),
                          pl.BlockSpec((1, 1, 2 * D_FF), expert_block),
                          pl.BlockSpec((1, D_FF, d), expert_block),
                          pl.BlockSpec((1, 1, d), expert_block)],
                out_specs=any_spec,
                scratch_shapes=[pltpu.VMEM((2, MOE_ROWS * SLABS, LANES), F32),
                                pltpu.VMEM((2, MOE_ROWS * SLABS, LANES), F32),
                                pltpu.SMEM((2 * MOE_ROWS,), jnp.int32), pltpu.SMEM((2 * MOE_ROWS,), jnp.int32),
                                pltpu.VMEM((d, 2 * D_FF), BF16), pltpu.VMEM((D_FF, d), BF16),
                                pltpu.SemaphoreType.DMA(()), pltpu.SemaphoreType.DMA(()),
                                pltpu.SemaphoreType.DMA(())]),
            out_shape=jax.ShapeDtypeStruct((n_slots * SLABS, LANES), F32),
            compiler_params=arb1,
            name="experts",
        )(meta, idx, h2, w_gu[l], b_gu[l].reshape(N_EXPERTS, 1, 2 * D_FF), w_down[l],
          b_down[l].reshape(N_EXPERTS, 1, d))

        out = pl.pallas_call(
            _combine_kernel,
            grid=(n_tiles,),
            in_specs=[pl.BlockSpec((TOK_TILE, d), lambda i: (i, 0)),
                      pl.BlockSpec((TOK_TILE, TOP_K), lambda i: (i, 0)),
                      pl.BlockSpec((1, 6, d), lambda i: (i // (seq // TOK_TILE), 0, 0)),
                      pl.BlockSpec((1, d), lambda i: (0, 0)),
                      pl.BlockSpec((TOK_TILE * TOP_K * SLABS, LANES), lambda i: (i, 0))],
            out_specs=pl.BlockSpec((TOK_TILE, d), lambda i: (i, 0)),
            out_shape=jax.ShapeDtypeStruct((n_tok, d), F32),
            compiler_params=arb1,
            name="combine",
        )(x1.reshape(n_tok, d), gates, mod, g_final.reshape(1, d), y)
        x = out.reshape(bsz, seq, d)
    return x
```

```python
import functools

import jax
import numpy as np
import jax.numpy as jnp
from jax import lax
from jax.experimental import pallas as pl
from jax.experimental.pallas import tpu as pltpu

F32 = jnp.float32
BF16 = jnp.bfloat16

D_MODEL = 1024
N_HEADS = 8
HEAD_DIM = 64
ATTN_WIDTH = N_HEADS * HEAD_DIM
CONV_CH = D_MODEL - ATTN_WIDTH
IN_COLS = 2 * CONV_CH + 3 * ATTN_WIDTH
CONV_WIDTH = 31
ROT_DIM = HEAD_DIM // 4
ROPE_THETA = 500000.0
DILATIONS = (1, 4, 16)
ATTN_BLK = 128
ATTN_GROUP = 16
N_EXPERTS = 32
TOP_K = 4
D_FF = D_MODEL
SWIGLU_LIMIT = 7.0
SWIGLU_ALPHA = 1.702
NORM_EPS = 1e-6

LANES = 128
SUBLANES = 8
SLABS = D_MODEL // LANES
VMEM_LIMIT = 56 * 1024 * 1024

PROJ_ROWS = 512
CONV_HALO = 32
CONV_ROWS = 64
ATTN_CHUNK = 2048
MOE_ROWS = 512
MOE_COLS = 256
TOK_TILE = 256


def _ada_kernel(c_ref, w_ref, b_ref, o_ref):
    c = c_ref[...]
    c_act = c * jax.nn.sigmoid(c)
    o_ref[...] = jnp.dot(c_act.astype(BF16), w_ref[...].astype(BF16), preferred_element_type=F32) + b_ref[...]


def _rms_modulate(x, gain, shift, scale):
    ms = jnp.mean(x * x, axis=-1, keepdims=True)
    y = x * lax.rsqrt(ms + NORM_EPS) * gain
    return y * (1.0 + scale) + shift


def _proj_kernel(x_ref, mod_ref, gmix_ref, win_ref, perm_ref, cw_ref, cb_ref, lng_ref, lnb_ref,
                 rc_ref, rsa_ref, rsb_ref, co_ref, q_ref, k_ref, v_ref, ubuf):
    rows = x_ref.shape[1]
    hb = _rms_modulate(x_ref[0], gmix_ref[...], mod_ref[0, 0:1, :], mod_ref[0, 1:2, :]).astype(BF16)

    ag = jnp.dot(hb, win_ref[:, :2 * CONV_CH], preferred_element_type=F32)
    u = ag[:, :CONV_CH] * jax.nn.sigmoid(ag[:, CONV_CH:])

    @pl.when(pl.program_id(1) == 0)
    def _():
        ubuf[0:CONV_HALO, :] = jnp.zeros((CONV_HALO, CONV_CH), F32)

    ubuf[CONV_HALO:, :] = u
    first_tap = CONV_HALO - (CONV_WIDTH - 1)
    for ch in range(rows // CONV_ROWS):
        row0 = ch * CONV_ROWS
        acc = jnp.broadcast_to(cb_ref[...], (CONV_ROWS, CONV_CH))
        for shift in range(SUBLANES):
            taps = [j for j in range(CONV_WIDTH) if (first_tap + j) % SUBLANES == shift]
            span = CONV_ROWS + (SUBLANES if shift else 0)
            part = None
            for j in taps:
                base = row0 + first_tap + j - shift
                term = cw_ref[j:j + 1, :] * ubuf[base:base + span, :]
                part = term if part is None else part + term
            acc = acc + part[shift:shift + CONV_ROWS]
        mu = jnp.mean(acc, axis=-1, keepdims=True)
        cen = acc - mu
        var = jnp.mean(cen * cen, axis=-1, keepdims=True)
        yn = cen * lax.rsqrt(var + NORM_EPS) * lng_ref[...] + lnb_ref[...]
        co_ref[0, ch * CONV_ROWS:(ch + 1) * CONV_ROWS, :] = (yn * jax.nn.sigmoid(yn)).astype(co_ref.dtype)
    ubuf[0:CONV_HALO, :] = ubuf[rows:rows + CONV_HALO, :]

    hp = jnp.dot(perm_ref[...], hb, preferred_element_type=F32).astype(BF16)
    qkv = jnp.dot(hp, win_ref[:, 2 * CONV_CH:], preferred_element_type=F32)
    run = rows // ATTN_GROUP
    tile = (ATTN_GROUP, run, LANES)
    rc, rsa, rsb = (t[0, 0].reshape(rows, LANES) for t in (rc_ref, rsa_ref, rsb_ref))
    lane = lax.broadcasted_iota(jnp.int32, (rows, LANES), 1)
    ones_col = jnp.where(lane == HEAD_DIM, 1.0, 0.0)
    half = ROT_DIM // 2
    for blk in range(ATTN_WIDTH // LANES):
        lo, hi = blk * LANES, (blk + 1) * LANES
        for src0, dst_ref, scale in ((0, q_ref, HEAD_DIM ** -0.5), (ATTN_WIDTH, k_ref, 1.0)):
            t = qkv[:, src0 + lo:src0 + hi]
            rot = t * rc + pltpu.roll(t, LANES - half, 1) * rsa + pltpu.roll(t, half, 1) * rsb
            dst_ref[0, 0, :, :, lo:hi] = (rot * scale).reshape(tile)
        vb = qkv[:, 2 * ATTN_WIDTH + lo:2 * ATTN_WIDTH + hi]
        for hsel, vals in enumerate((vb, pltpu.roll(vb, HEAD_DIM, 1))):
            vh = jnp.where(lane < HEAD_DIM, vals, ones_col)
            v_ref[0, 0, :, :, (2 * blk + hsel) * LANES:(2 * blk + hsel + 1) * LANES] = vh.reshape(tile)


def _attn_kernel(bias_ref, q_ref, kp_ref, kc_ref, vp_ref, vc_ref, o_ref, st, st_max):
    chunk = pl.program_id(2)
    lane = lax.broadcasted_iota(jnp.int32, (ATTN_BLK, LANES), 1)
    head0 = lane < HEAD_DIM
    kcol = lax.broadcasted_iota(jnp.int32, (ATTN_BLK, 2 * ATTN_BLK), 1)
    neg_inf = jnp.float32(-jnp.inf)
    n_blocks = ATTN_CHUNK // ATTN_BLK
    all_lanes = slice(None)

    for pi, d in enumerate(DILATIONS):
        n_runs = ATTN_GROUP // d
        run = ATTN_BLK // n_runs
        n_sub = n_blocks // d
        bias = bias_ref[pi]
        bias_first = jnp.where(jnp.logical_or(kcol >= ATTN_BLK, chunk > 0), bias, neg_inf)
        for idx in range(n_blocks):
            u, res = idx // d, idx % d
            a_cur = u * run
            prev_k, prev_v, a_prev = (kc_ref, vc_ref, a_cur - run) if u > 0 else (kp_ref, vp_ref, (n_sub - 1) * run)

            def rows_of(ref, a0, lanes):
                parts = [ref[0, 0, i * d + res, a0:a0 + run, lanes] for i in range(n_runs)]
                return parts[0] if n_runs == 1 else jnp.concatenate(parts, axis=0)

            qb = rows_of(q_ref, a_cur, all_lanes)
            kb = jnp.concatenate([rows_of(prev_k, a_prev, all_lanes),
                                  rows_of(kc_ref, a_cur, all_lanes)], axis=0).astype(BF16)
            q2 = jnp.concatenate([jnp.where(head0, qb, 0.0), jnp.where(head0, 0.0, qb)], axis=0).astype(BF16)
            s2 = lax.dot_general(q2, kb, (((1,), (1,)), ((), ())), preferred_element_type=F32)
            for hsel in range(2):
                s = s2[hsel * ATTN_BLK:(hsel + 1) * ATTN_BLK] + (bias if u > 0 else bias_first)
                m = jnp.max(s, axis=1, keepdims=True)
                p = jnp.exp(s - m).astype(BF16)
                lanes = slice(hsel * LANES, (hsel + 1) * LANES)
                vb = jnp.concatenate([rows_of(prev_v, a_prev, lanes),
                                      rows_of(vc_ref, a_cur, lanes)], axis=0).astype(BF16)
                pv = jnp.dot(p, vb, preferred_element_type=F32)
                mfull = jnp.broadcast_to(m, (ATTN_BLK, LANES))
                for i in range(n_runs):
                    st[pi, hsel, i * d + res, a_cur:a_cur + run, :] = pv[i * run:(i + 1) * run]
                    st_max[pi, hsel, i * d + res, a_cur:a_cur + run, :] = mfull[i * run:(i + 1) * run]

    def merge(g, carry):
        outs = []
        for hsel in range(2):
            maxes = [st_max[pi, hsel, g] for pi in range(len(DILATIONS))]
            top = functools.reduce(jnp.maximum, maxes)
            tot = sum(jnp.exp(mx - top) * st[pi, hsel, g] for pi, mx in enumerate(maxes))
            outs.append(tot / tot[:, HEAD_DIM:HEAD_DIM + 1])
        o_ref[0, 0, g] = jnp.where(head0, outs[0], pltpu.roll(outs[1], HEAD_DIM, 1)).astype(o_ref.dtype)
        return carry

    lax.fori_loop(0, ATTN_GROUP, merge, 0, unroll=2)


def _mix_kernel(co_ref, ao_ref, x_ref, mod_ref, wo_ref, unperm_ref, gffn_ref, wr_ref, br_ref,
                x1_ref, h2_ref, lg_ref):
    rows = x_ref.shape[1]
    ao = jnp.dot(unperm_ref[...], ao_ref[0, 0].reshape(rows, ATTN_WIDTH),
                 preferred_element_type=F32).astype(BF16)
    mixed = (jnp.dot(co_ref[0], wo_ref[0:CONV_CH, :], preferred_element_type=F32)
             + jnp.dot(ao, wo_ref[CONV_CH:, :], preferred_element_type=F32))
    x1 = x_ref[0] + mod_ref[0, 2:3, :] * mixed
    x1_ref[0] = x1
    h2 = _rms_modulate(x1, gffn_ref[...], mod_ref[0, 3:4, :], mod_ref[0, 4:5, :])
    for j in range(SLABS):
        h2_ref[pl.ds(j, rows, stride=SLABS), :] = h2[:, j * LANES:(j + 1) * LANES]
    lg_ref[...] = jnp.dot(h2.astype(BF16), wr_ref[...], preferred_element_type=F32) + br_ref[...]


def _slab_rows(idx):
    return pl.ds(pl.multiple_of(idx * SLABS, SLABS), SLABS)


def _expert_kernel(meta_ref, idx_hbm, h2_hbm, wgu_ref, bgu_ref, wd_ref, bd_ref, y_hbm,
                   xbuf, ybuf, idx_a, idx_b, wgu_bf, wd_bf, gsem, ssem, isem):
    i = pl.program_id(0)
    n_used = meta_ref[0]
    expert = meta_ref[1 + i]
    prev_expert = meta_ref[jnp.maximum(i, 1)]
    cur = i % 2
    idx_bufs = (idx_a, idx_b)
    idx_row = 2 * MOE_ROWS

    def idx_copy(row, slot):
        start = pl.multiple_of(row * idx_row, idx_row)
        return pltpu.make_async_copy(idx_hbm.at[pl.ds(start, idx_row)], idx_bufs[slot], isem)

    def row_slab(r):
        return pl.ds(r * SLABS, SLABS) if isinstance(r, int) else _slab_rows(r)

    def gather_copy(idx_slot, buf_slot, r):
        return pltpu.make_async_copy(h2_hbm.at[_slab_rows(idx_bufs[idx_slot][r]), :],
                                     xbuf.at[buf_slot, row_slab(r), :], gsem)

    def scatter_copy(idx_slot, buf_slot, r):
        return pltpu.make_async_copy(ybuf.at[buf_slot, row_slab(r), :],
                                     y_hbm.at[_slab_rows(idx_bufs[idx_slot][MOE_ROWS + r]), :], ssem)

    def start_rows(idx_slot, buf_slot, gather, scatter, unrolled):
        def body(r, carry):
            if gather:
                gather_copy(idx_slot, buf_slot, r).start()
            if scatter:
                scatter_copy(idx_slot, buf_slot, r).start()
            return carry
        if unrolled:
            for r in range(MOE_ROWS):
                body(r, 0)
        else:
            lax.fori_loop(0, MOE_ROWS, body, 0, unroll=8)

    def wait_gathers():
        pltpu.make_async_copy(h2_hbm.at[pl.ds(0, MOE_ROWS * SLABS), :], xbuf.at[0], gsem).wait()

    def wait_scatters():
        pltpu.make_async_copy(ybuf.at[0], y_hbm.at[pl.ds(0, MOE_ROWS * SLABS), :], ssem).wait()

    @pl.when(i == 0)
    def _():
        first = idx_copy(0, 1)
        first.start()
        first.wait()
        start_rows(1, 0, True, False, False)
        wait_gathers()
        ybuf[1] = jnp.zeros((MOE_ROWS * SLABS, LANES), F32)
        second = idx_copy(1, 0)
        second.start()
        second.wait()

    @pl.when(jnp.logical_or(i == 0, expert != prev_expert))
    def _():
        wgu_bf[...] = wgu_ref[0].astype(BF16)
        wd_bf[...] = wd_ref[0].astype(BF16)

    for parity in (0, 1):
        @pl.when(jnp.logical_and(i < n_used, cur == parity))
        def _(parity=parity):
            idx_copy(i + 2, 1 - parity).start()
            start_rows(parity, 1 - parity, True, True, True)

        @pl.when(jnp.logical_and(i == n_used, cur == parity))
        def _(parity=parity):
            start_rows(parity, 1 - parity, False, True, False)
            wait_scatters()

    @pl.when(i < n_used)
    def _():
        x = jnp.concatenate([xbuf[cur, pl.ds(j, MOE_ROWS, stride=SLABS), :] for j in range(SLABS)],
                            axis=1).astype(BF16)
        y = jnp.broadcast_to(bd_ref[0], (MOE_ROWS, D_MODEL))
        for cidx in range(D_FF // MOE_COLS):
            lo = cidx * MOE_COLS
            gate = jnp.dot(x, wgu_bf[:, lo:lo + MOE_COLS], preferred_element_type=F32) + bgu_ref[0, :, lo:lo + MOE_COLS]
            up = (jnp.dot(x, wgu_bf[:, D_FF + lo:D_FF + lo + MOE_COLS], preferred_element_type=F32)
                  + bgu_ref[0, :, D_FF + lo:D_FF + lo + MOE_COLS])
            gate = jnp.minimum(gate, SWIGLU_LIMIT)
            up = jnp.clip(up, -SWIGLU_LIMIT, SWIGLU_LIMIT)
            act = (up + 1.0) * (gate * jax.nn.sigmoid(SWIGLU_ALPHA * gate))
            y = y + jnp.dot(act.astype(BF16), wd_bf[lo:lo + MOE_COLS, :], preferred_element_type=F32)
        for j in range(SLABS):
            ybuf[cur, pl.ds(j, MOE_ROWS, stride=SLABS), :] = y[:, j * LANES:(j + 1) * LANES]
        wait_gathers()
        wait_scatters()
        idx_copy(0, 0).wait()


def _combine_kernel(gate_ref, x1_ref, mod_ref, gfin_ref, y_ref, o_ref, slab):
    base = pl.program_id(0) * (TOK_TILE * TOP_K)

    def token(t, carry):
        acc = None
        for k in range(TOP_K):
            a = t * TOP_K + k
            term = gate_ref[base + a] * y_ref[_slab_rows(a), :]
            acc = term if acc is None else acc + term
        slab[_slab_rows(t), :] = acc
        return carry

    lax.fori_loop(0, TOK_TILE, token, 0, unroll=8)
    ffn = jnp.concatenate([slab[pl.ds(j, TOK_TILE, stride=SLABS), :] for j in range(SLABS)], axis=1)
    x2 = x1_ref[...] + mod_ref[0, 5:6, :] * ffn
    ms = jnp.mean(x2 * x2, axis=-1, keepdims=True)
    o_ref[...] = x2 * lax.rsqrt(ms + NORM_EPS) * gfin_ref[...]


def _rotary_lane_tables(positions):
    half = ROT_DIM // 2
    inv_freq = ROPE_THETA ** (-jnp.arange(0, ROT_DIM, 2, dtype=F32) / ROT_DIM)
    ang = inv_freq.reshape((half,) + (1,) * positions.ndim) * positions.astype(F32)[None]
    cos, sin = jnp.moveaxis(jnp.cos(ang), 0, -1), jnp.moveaxis(jnp.sin(ang), 0, -1)
    ones = jnp.ones(cos.shape[:-1] + (HEAD_DIM - ROT_DIM,), F32)
    zeros = jnp.zeros_like(ones)
    zero_h = jnp.zeros_like(sin)
    rc = jnp.concatenate([cos, cos, ones], axis=-1)
    rsa = jnp.concatenate([-sin, zero_h, zeros], axis=-1)
    rsb = jnp.concatenate([zero_h, sin, zeros], axis=-1)
    reps = LANES // HEAD_DIM
    return tuple(jnp.tile(t, (1,) * positions.ndim + (reps,)) for t in (rc, rsa, rsb))


def _group_perm(rows):
    run = rows // ATTN_GROUP
    dst = np.arange(rows)
    perm = np.zeros((rows, rows), np.float32)
    perm[dst, ATTN_GROUP * (dst % run) + dst // run] = 1.0
    return jnp.asarray(perm, BF16)


def _band_bias():
    pos = np.arange(ATTN_BLK)
    tables = []
    for d in DILATIONS:
        n_runs = ATTN_GROUP // d
        run = ATTN_BLK // n_runs
        strided = (pos % run) * n_runs + pos // run
        dist = strided[:, None] + ATTN_BLK - np.concatenate([strided, strided + ATTN_BLK])[None, :]
        tables.append(np.where((dist >= 0) & (dist <= ATTN_BLK), 0.0, -np.inf))
    return jnp.asarray(np.stack(tables), F32)


def _attention(q, k, v):
    bsz, n_chunks, _, grp_rows, _ = q.shape

    def chunk_spec(width, back):
        return pl.BlockSpec((1, 1, ATTN_GROUP, grp_rows, width),
                            lambda b, p, ch: (b, jnp.maximum(ch - back, 0), 0, 0, p))
    return pl.pallas_call(
        _attn_kernel,
        grid=(bsz, ATTN_WIDTH // LANES, n_chunks),
        in_specs=[pl.BlockSpec((len(DILATIONS), ATTN_BLK, 2 * ATTN_BLK), lambda b, p, ch: (0, 0, 0)),
                  chunk_spec(LANES, 0), chunk_spec(LANES, 1), chunk_spec(LANES, 0),
                  chunk_spec(2 * LANES, 1), chunk_spec(2 * LANES, 0)],
        out_specs=chunk_spec(LANES, 0),
        out_shape=jax.ShapeDtypeStruct(q.shape, BF16),
        scratch_shapes=[pltpu.VMEM((len(DILATIONS), 2, ATTN_GROUP, grp_rows, LANES), F32),
                        pltpu.VMEM((len(DILATIONS), 2, ATTN_GROUP, grp_rows, LANES), F32)],
        compiler_params=pltpu.CompilerParams(dimension_semantics=("arbitrary",) * 3,
                                             vmem_limit_bytes=VMEM_LIMIT),
        name="attn",
    )(_band_bias(), q, k, k, v, v)


def _experts(meta, idx, h2, w_gu, b_gu, w_down, b_down, n_blocks, n_slots):
    d = w_down.shape[-1]

    def expert_block(i, meta):
        return (meta[1 + i], 0, 0)

    any_spec = pl.BlockSpec(memory_space=pl.ANY)
    return pl.pallas_call(
        _expert_kernel,
        grid_spec=pltpu.PrefetchScalarGridSpec(
            num_scalar_prefetch=1,
            grid=(n_blocks + 1,),
            in_specs=[any_spec, any_spec,
                      pl.BlockSpec((1, d, 2 * D_FF), expert_block),
                      pl.BlockSpec((1, 1, 2 * D_FF), expert_block),
                      pl.BlockSpec((1, D_FF, d), expert_block),
                      pl.BlockSpec((1, 1, d), expert_block)],
            out_specs=any_spec,
            scratch_shapes=[pltpu.VMEM((2, MOE_ROWS * SLABS, LANES), F32),
                            pltpu.VMEM((2, MOE_ROWS * SLABS, LANES), F32),
                            pltpu.SMEM((2 * MOE_ROWS,), jnp.int32), pltpu.SMEM((2 * MOE_ROWS,), jnp.int32),
                            pltpu.VMEM((d, 2 * D_FF), BF16), pltpu.VMEM((D_FF, d), BF16),
                            pltpu.SemaphoreType.DMA(()), pltpu.SemaphoreType.DMA(()),
                            pltpu.SemaphoreType.DMA(())]),
        out_shape=jax.ShapeDtypeStruct((n_slots * SLABS, LANES), F32),
        compiler_params=pltpu.CompilerParams(dimension_semantics=("arbitrary",), vmem_limit_bytes=VMEM_LIMIT),
        name="experts",
    )(meta, idx, h2, w_gu, b_gu.reshape(N_EXPERTS, 1, 2 * D_FF), w_down, b_down.reshape(N_EXPERTS, 1, d))


def _routing(logits):
    n_tok = logits.shape[0]
    n_asg = n_tok * TOP_K
    i32 = jnp.int32
    top_val, top_idx = lax.top_k(logits, TOP_K)
    gates = jax.nn.softmax(top_val, axis=-1)
    key = top_idx.reshape(n_asg).astype(i32) * n_asg + jnp.arange(n_asg, dtype=i32)
    skey = lax.sort(key)
    asg_sorted = skey % n_asg
    experts = jnp.arange(N_EXPERTS, dtype=i32)
    counts = jnp.sum((top_idx.reshape(n_asg, 1) == experts[None, :]).astype(i32), axis=0)
    starts = jnp.cumsum(counts) - counts
    padded = (counts + MOE_ROWS - 1) // MOE_ROWS * MOE_ROWS
    pends = jnp.cumsum(padded)
    pstarts = pends - padded
    n_blocks = n_asg // MOE_ROWS + N_EXPERTS
    n_used = (pends[-1] // MOE_ROWS).astype(i32)
    blk = jnp.arange(n_blocks + 2, dtype=i32)
    blk_start = blk * MOE_ROWS
    clamped = jnp.minimum(blk_start, pends[-1] - MOE_ROWS)
    blk_expert = jnp.sum((pends[None, :] <= clamped[:, None]).astype(i32), axis=1)
    row = jnp.arange(MOE_ROWS, dtype=i32)[None, :]
    off = blk_start[:, None] + row - pstarts[blk_expert][:, None]
    valid = jnp.logical_and(off < counts[blk_expert][:, None], blk[:, None] < n_used)
    asg = asg_sorted[jnp.clip(starts[blk_expert][:, None] + off, 0, n_asg - 1)]
    tok = jnp.where(valid, asg // TOP_K, 0)
    spare = jnp.broadcast_to(n_asg + row, (2, MOE_ROWS))
    slot = jnp.concatenate([spare, jnp.where(valid, asg, n_asg + row)[:-2]], axis=0)
    idx = jnp.concatenate([tok, slot], axis=1).reshape(-1).astype(i32)
    meta = jnp.concatenate([n_used[None], blk_expert[:n_blocks + 1]])
    return gates, idx, meta, n_blocks


def kernel(x, c, positions, w_ada, b_ada, g_mix, w_in, conv_w, conv_b, conv_ln_g, conv_ln_b,
           w_out, g_ffn, w_router, b_router, w_gu, b_gu, w_down, b_down, g_final):
    bsz, seq, d = x.shape
    n_tok = bsz * seq
    depth = w_ada.shape[0]
    assert depth == 1, "the final RMSNorm is fused into the layer's combine kernel"
    n_chunks = seq // ATTN_CHUNK
    grp_rows = ATTN_CHUNK // ATTN_GROUP
    tile_run = PROJ_ROWS // ATTN_GROUP
    tiles_per_chunk = ATTN_CHUNK // PROJ_ROWS
    grouped_pos = positions.reshape(bsz, n_chunks, grp_rows, ATTN_GROUP).swapaxes(2, 3)
    rc, rsa, rsb = _rotary_lane_tables(grouped_pos)
    perm = _group_perm(PROJ_ROWS)
    s_tiles = seq // PROJ_ROWS
    arb2 = pltpu.CompilerParams(dimension_semantics=("arbitrary", "arbitrary"), vmem_limit_bytes=VMEM_LIMIT)
    arb1 = pltpu.CompilerParams(dimension_semantics=("arbitrary",), vmem_limit_bytes=VMEM_LIMIT)

    for l in range(depth):
        n_mod = 6 * d
        mod = pl.pallas_call(
            _ada_kernel,
            grid=(n_mod // d,),
            in_specs=[pl.BlockSpec((bsz, d), lambda j: (0, 0)),
                      pl.BlockSpec((d, d), lambda j: (0, j)),
                      pl.BlockSpec((1, d), lambda j: (0, j))],
            out_specs=pl.BlockSpec((bsz, d), lambda j: (0, j)),
            out_shape=jax.ShapeDtypeStruct((bsz, n_mod), F32),
            name="ada",
        )(c, w_ada[l], b_ada[l].reshape(1, n_mod))
        mod = mod.reshape(bsz, 6, d)

        row_spec = lambda width: pl.BlockSpec((1, PROJ_ROWS, width), lambda b, s: (b, s, 0))
        grp_spec = lambda width: pl.BlockSpec(
            (1, 1, ATTN_GROUP, tile_run, width), lambda b, s: (b, s // tiles_per_chunk, 0, s % tiles_per_chunk, 0))
        grp_shape = lambda width, dtype: jax.ShapeDtypeStruct((bsz, n_chunks, ATTN_GROUP, grp_rows, width), dtype)
        full2 = lambda shape: pl.BlockSpec(shape, lambda b, s: (0, 0))
        mod_spec = pl.BlockSpec((1, 6, d), lambda b, s: (b, 0, 0))
        conv_out, q, k, v = pl.pallas_call(
            _proj_kernel,
            grid=(bsz, s_tiles),
            in_specs=[row_spec(d), mod_spec, full2((1, d)), full2((d, IN_COLS)), full2((PROJ_ROWS, PROJ_ROWS)),
                      full2((CONV_WIDTH, CONV_CH)), full2((1, CONV_CH)), full2((1, CONV_CH)), full2((1, CONV_CH)),
                      grp_spec(LANES), grp_spec(LANES), grp_spec(LANES)],
            out_specs=[row_spec(CONV_CH), grp_spec(ATTN_WIDTH), grp_spec(ATTN_WIDTH), grp_spec(2 * ATTN_WIDTH)],
            out_shape=[jax.ShapeDtypeStruct((bsz, seq, CONV_CH), BF16),
                       grp_shape(ATTN_WIDTH, F32), grp_shape(ATTN_WIDTH, F32), grp_shape(2 * ATTN_WIDTH, F32)],
            scratch_shapes=[pltpu.VMEM((PROJ_ROWS + CONV_HALO, CONV_CH), F32)],
            compiler_params=arb2,
            name="proj",
        )(x, mod, g_mix[l].reshape(1, d), w_in[l].astype(BF16), perm, conv_w[l].reshape(CONV_WIDTH, CONV_CH),
          conv_b[l].reshape(1, CONV_CH), conv_ln_g[l].reshape(1, CONV_CH), conv_ln_b[l].reshape(1, CONV_CH),
          rc, rsa, rsb)

        attn_out = _attention(q, k, v)

        x1, h2, logits = pl.pallas_call(
            _mix_kernel,
            grid=(bsz, s_tiles),
            in_specs=[row_spec(CONV_CH), grp_spec(ATTN_WIDTH), row_spec(d), mod_spec, full2((d, d)),
                      full2((PROJ_ROWS, PROJ_ROWS)), full2((1, d)), full2((d, N_EXPERTS)), full2((1, N_EXPERTS))],
            out_specs=[row_spec(d),
                       pl.BlockSpec((PROJ_ROWS * SLABS, LANES), lambda b, s: (b * s_tiles + s, 0)),
                       pl.BlockSpec((PROJ_ROWS, N_EXPERTS), lambda b, s: (b * s_tiles + s, 0))],
            out_shape=[jax.ShapeDtypeStruct((bsz, seq, d), F32),
                       jax.ShapeDtypeStruct((n_tok * SLABS, LANES), F32),
                       jax.ShapeDtypeStruct((n_tok, N_EXPERTS), F32)],
            compiler_params=arb2,
            name="mix",
        )(conv_out, attn_out, x, mod, w_out[l].astype(BF16), perm.T, g_ffn[l].reshape(1, d),
          w_router[l].astype(BF16), b_router[l].reshape(1, N_EXPERTS))

        gates, idx, meta, n_blocks = _routing(logits)
        n_slots = n_tok * TOP_K + MOE_ROWS
        n_tiles = n_tok // TOK_TILE

        y = _experts(meta, idx, h2, w_gu[l], b_gu[l], w_down[l], b_down[l], n_blocks, n_slots)

        out = pl.pallas_call(
            _combine_kernel,
            grid_spec=pltpu.PrefetchScalarGridSpec(
                num_scalar_prefetch=1,
                grid=(n_tiles,),
                in_specs=[pl.BlockSpec((TOK_TILE, d), lambda i, g: (i, 0)),
                          pl.BlockSpec((1, 6, d), lambda i, g: (i // (seq // TOK_TILE), 0, 0)),
                          pl.BlockSpec((1, d), lambda i, g: (0, 0)),
                          pl.BlockSpec((TOK_TILE * TOP_K * SLABS, LANES), lambda i, g: (i, 0))],
                out_specs=pl.BlockSpec((TOK_TILE, d), lambda i, g: (i, 0)),
                scratch_shapes=[pltpu.VMEM((TOK_TILE * SLABS, LANES), F32)]),
            out_shape=jax.ShapeDtypeStruct((n_tok, d), F32),
            compiler_params=arb1,
            name="combine",
        )(gates.reshape(-1), x1.reshape(n_tok, d), mod, g_final.reshape(1, d), y)
        x = out.reshape(bsz, seq, d)
    return x
```

```python
import functools

import jax
import numpy as np
import jax.numpy as jnp
from jax import lax
from jax.experimental import pallas as pl
from jax.experimental.pallas import tpu as pltpu

F32 = jnp.float32
BF16 = jnp.bfloat16

D_MODEL = 1024
N_HEADS = 8
HEAD_DIM = 64
ATTN_WIDTH = N_HEADS * HEAD_DIM
CONV_CH = D_MODEL - ATTN_WIDTH
IN_COLS = 2 * CONV_CH + 3 * ATTN_WIDTH
CONV_WIDTH = 31
ROT_DIM = HEAD_DIM // 4
ROPE_THETA = 500000.0
DILATIONS = (1, 4, 16)
ATTN_BLK = 128
ATTN_GROUP = 16
N_EXPERTS = 32
TOP_K = 4
D_FF = D_MODEL
SWIGLU_LIMIT = 7.0
SWIGLU_ALPHA = 1.702
NORM_EPS = 1e-6

LANES = 128
SUBLANES = 8
SLABS = D_MODEL // LANES
VMEM_LIMIT = 56 * 1024 * 1024

PROJ_ROWS = 512
CONV_HALO = 32
CONV_ROWS = 64
ATTN_CHUNK = 2048
MOE_ROWS = 512
MOE_COLS = 512
TOK_TILE = 256


def _ada_kernel(c_ref, w_ref, b_ref, o_ref):
    c = c_ref[...]
    c_act = c * jax.nn.sigmoid(c)
    o_ref[...] = jnp.dot(c_act.astype(BF16), w_ref[...].astype(BF16), preferred_element_type=F32) + b_ref[...]


def _rms_modulate(x, gain, shift, scale):
    ms = jnp.mean(x * x, axis=-1, keepdims=True)
    y = x * lax.rsqrt(ms + NORM_EPS) * gain
    return y * (1.0 + scale) + shift


def _proj_kernel(x_ref, mod_ref, gmix_ref, win_ref, perm_ref, cw_ref, cb_ref, lng_ref, lnb_ref,
                 rc_ref, rsa_ref, rsb_ref, co_ref, q_ref, k_ref, v_ref, ubuf):
    rows = x_ref.shape[1]
    hb = _rms_modulate(x_ref[0], gmix_ref[...], mod_ref[0, 0:1, :], mod_ref[0, 1:2, :]).astype(BF16)

    ag = jnp.dot(hb, win_ref[:, :2 * CONV_CH], preferred_element_type=F32)
    u = ag[:, :CONV_CH] * jax.nn.sigmoid(ag[:, CONV_CH:])

    @pl.when(pl.program_id(1) == 0)
    def _():
        ubuf[0:CONV_HALO, :] = jnp.zeros((CONV_HALO, CONV_CH), F32)

    ubuf[CONV_HALO:, :] = u
    first_tap = CONV_HALO - (CONV_WIDTH - 1)
    for ch in range(rows // CONV_ROWS):
        row0 = ch * CONV_ROWS
        acc = jnp.broadcast_to(cb_ref[...], (CONV_ROWS, CONV_CH))
        for shift in range(SUBLANES):
            taps = [j for j in range(CONV_WIDTH) if (first_tap + j) % SUBLANES == shift]
            span = CONV_ROWS + (SUBLANES if shift else 0)
            part = None
            for j in taps:
                base = row0 + first_tap + j - shift
                term = cw_ref[j:j + 1, :] * ubuf[base:base + span, :]
                part = term if part is None else part + term
            acc = acc + part[shift:shift + CONV_ROWS]
        mu = jnp.mean(acc, axis=-1, keepdims=True)
        cen = acc - mu
        var = jnp.mean(cen * cen, axis=-1, keepdims=True)
        yn = cen * lax.rsqrt(var + NORM_EPS) * lng_ref[...] + lnb_ref[...]
        co_ref[0, ch * CONV_ROWS:(ch + 1) * CONV_ROWS, :] = (yn * jax.nn.sigmoid(yn)).astype(co_ref.dtype)
    ubuf[0:CONV_HALO, :] = ubuf[rows:rows + CONV_HALO, :]

    hp = jnp.dot(perm_ref[...], hb, preferred_element_type=F32).astype(BF16)
    qkv = jnp.dot(hp, win_ref[:, 2 * CONV_CH:], preferred_element_type=F32)
    run = rows // ATTN_GROUP
    tile = (ATTN_GROUP, run, LANES)
    rc, rsa, rsb = (t[0, 0].reshape(rows, LANES) for t in (rc_ref, rsa_ref, rsb_ref))
    half = ROT_DIM // 2
    for blk in range(ATTN_WIDTH // LANES):
        lo, hi = blk * LANES, (blk + 1) * LANES
        for src0, dst_ref, scale in ((0, q_ref, HEAD_DIM ** -0.5), (ATTN_WIDTH, k_ref, 1.0)):
            t = qkv[:, src0 + lo:src0 + hi]
            rot = t * rc + pltpu.roll(t, LANES - half, 1) * rsa + pltpu.roll(t, half, 1) * rsb
            dst_ref[0, 0, :, :, lo:hi] = (rot * scale).reshape(tile)
        v_ref[0, 0, :, :, lo:hi] = qkv[:, 2 * ATTN_WIDTH + lo:2 * ATTN_WIDTH + hi].reshape(tile)


def _attn_kernel(bias_ref, q_ref, kp_ref, kc_ref, vp_ref, vc_ref, o_ref, st):
    chunk = pl.program_id(2)
    lane = lax.broadcasted_iota(jnp.int32, (ATTN_BLK, LANES), 1)
    head0 = lane < HEAD_DIM
    kcol = lax.broadcasted_iota(jnp.int32, (ATTN_BLK, 2 * ATTN_BLK), 1)
    neg_inf = jnp.float32(-jnp.inf)
    n_blocks = ATTN_CHUNK // ATTN_BLK
    all_lanes = slice(None)

    for pi, d in enumerate(DILATIONS):
        n_runs = ATTN_GROUP // d
        run = ATTN_BLK // n_runs
        n_sub = n_blocks // d
        bias = bias_ref[pi]
        bias_first = jnp.where(jnp.logical_or(kcol >= ATTN_BLK, chunk > 0), bias, neg_inf)
        for idx in range(n_blocks):
            u, res = idx // d, idx % d
            a_cur = u * run
            prev_k, prev_v, a_prev = (kc_ref, vc_ref, a_cur - run) if u > 0 else (kp_ref, vp_ref, (n_sub - 1) * run)

            def rows_of(ref, a0, lanes):
                parts = [ref[0, 0, i * d + res, a0:a0 + run, lanes] for i in range(n_runs)]
                return parts[0] if n_runs == 1 else jnp.concatenate(parts, axis=0)

            qb = rows_of(q_ref, a_cur, all_lanes)
            kb = jnp.concatenate([rows_of(prev_k, a_prev, all_lanes),
                                  rows_of(kc_ref, a_cur, all_lanes)], axis=0).astype(BF16)
            q2 = jnp.concatenate([jnp.where(head0, qb, 0.0), jnp.where(head0, 0.0, qb)], axis=0).astype(BF16)
            s2 = lax.dot_general(q2, kb, (((1,), (1,)), ((), ())), preferred_element_type=F32)
            vb = jnp.concatenate([rows_of(prev_v, a_prev, all_lanes),
                                  rows_of(vc_ref, a_cur, all_lanes)], axis=0).astype(BF16)
            parts = []
            for hsel in range(2):
                s = s2[hsel * ATTN_BLK:(hsel + 1) * ATTN_BLK] + (bias if u > 0 else bias_first)
                m = jnp.max(s, axis=1, keepdims=True)
                p = jnp.exp(s - m)
                parts.append((jnp.dot(p.astype(BF16), vb, preferred_element_type=F32),
                              m, jnp.sum(p, axis=1, keepdims=True)))
            vals = [jnp.where(head0, a, b) for a, b in zip(*parts)]
            for i in range(n_runs):
                for si, val in enumerate(vals):
                    st[si, pi, i * d + res, a_cur:a_cur + run, :] = val[i * run:(i + 1) * run]

    def merge(g, carry):
        maxes = [st[1, pi, g] for pi in range(len(DILATIONS))]
        top = functools.reduce(jnp.maximum, maxes)
        weights = [jnp.exp(mx - top) for mx in maxes]
        num = sum(w * st[0, pi, g] for pi, w in enumerate(weights))
        den = sum(w * st[2, pi, g] for pi, w in enumerate(weights))
        o_ref[0, 0, g] = (num / den).astype(o_ref.dtype)
        return carry

    lax.fori_loop(0, ATTN_GROUP, merge, 0, unroll=2)


def _top_k_gates(logits):
    lane = lax.broadcasted_iota(jnp.int32, logits.shape, 1).astype(F32)
    vals = jnp.zeros_like(logits)
    idxs = jnp.zeros_like(logits)
    rest = logits
    for k in range(TOP_K):
        best = jnp.max(rest, axis=1, keepdims=True)
        where_best = jnp.min(jnp.where(rest == best, lane, float(N_EXPERTS)), axis=1, keepdims=True)
        vals = jnp.where(lane == k, best, vals)
        idxs = jnp.where(lane == k, where_best, idxs)
        rest = jnp.where(lane == where_best, -jnp.inf, rest)
    weights = jnp.where(lane < TOP_K, jnp.exp(vals - vals[:, 0:1]), 0.0)
    gates = weights / jnp.sum(weights, axis=1, keepdims=True)
    return idxs[:, :TOP_K].astype(jnp.int32), gates[:, :TOP_K]


def _mix_kernel(co_ref, ao_ref, x_ref, mod_ref, wo_ref, unperm_ref, gffn_ref, wr_ref, br_ref,
                x1_ref, h2_ref, idx_ref, gate_ref):
    rows = x_ref.shape[1]
    ao = jnp.dot(unperm_ref[...], ao_ref[0, 0].reshape(rows, ATTN_WIDTH),
                 preferred_element_type=F32).astype(BF16)
    mixed = (jnp.dot(co_ref[0], wo_ref[0:CONV_CH, :], preferred_element_type=F32)
             + jnp.dot(ao, wo_ref[CONV_CH:, :], preferred_element_type=F32))
    x1 = x_ref[0] + mod_ref[0, 2:3, :] * mixed
    x1_ref[0] = x1
    h2 = _rms_modulate(x1, gffn_ref[...], mod_ref[0, 3:4, :], mod_ref[0, 4:5, :])
    for j in range(SLABS):
        h2_ref[pl.ds(j, rows, stride=SLABS), :] = h2[:, j * LANES:(j + 1) * LANES]
    logits = jnp.dot(h2.astype(BF16), wr_ref[...], preferred_element_type=F32) + br_ref[...]
    idx_ref[...], gate_ref[...] = _top_k_gates(logits)


def _slab_rows(idx):
    return pl.ds(pl.multiple_of(idx * SLABS, SLABS), SLABS)


def _expert_kernel(meta_ref, idx_hbm, h2_hbm, wgu_ref, bgu_ref, wd_ref, bd_ref, y_hbm,
                   xbuf, ybuf, idx_a, idx_b, wgu_bf, wd_bf, gsem, ssem, isem):
    i = pl.program_id(0)
    n_used = meta_ref[0]
    expert = meta_ref[1 + i]
    prev_expert = meta_ref[jnp.maximum(i, 1)]
    cur = i % 2
    idx_bufs = (idx_a, idx_b)
    idx_row = 2 * MOE_ROWS

    def idx_copy(row, slot):
        start = pl.multiple_of(row * idx_row, idx_row)
        return pltpu.make_async_copy(idx_hbm.at[pl.ds(start, idx_row)], idx_bufs[slot], isem)

    def row_slab(r):
        return pl.ds(r * SLABS, SLABS) if isinstance(r, int) else _slab_rows(r)

    def gather_copy(idx_slot, buf_slot, r):
        return pltpu.make_async_copy(h2_hbm.at[_slab_rows(idx_bufs[idx_slot][r]), :],
                                     xbuf.at[buf_slot, row_slab(r), :], gsem)

    def scatter_copy(idx_slot, buf_slot, r):
        return pltpu.make_async_copy(ybuf.at[buf_slot, row_slab(r), :],
                                     y_hbm.at[_slab_rows(idx_bufs[idx_slot][MOE_ROWS + r]), :], ssem)

    def start_rows(idx_slot, buf_slot, gather, scatter, unrolled):
        def body(r, carry):
            if gather:
                gather_copy(idx_slot, buf_slot, r).start()
            if scatter:
                scatter_copy(idx_slot, buf_slot, r).start()
            return carry
        if unrolled:
            for r in range(MOE_ROWS):
                body(r, 0)
        else:
            lax.fori_loop(0, MOE_ROWS, body, 0, unroll=8)

    def wait_gathers():
        pltpu.make_async_copy(h2_hbm.at[pl.ds(0, MOE_ROWS * SLABS), :], xbuf.at[0], gsem).wait()

    def wait_scatters():
        pltpu.make_async_copy(ybuf.at[0], y_hbm.at[pl.ds(0, MOE_ROWS * SLABS), :], ssem).wait()

    @pl.when(i == 0)
    def _():
        first = idx_copy(0, 1)
        first.start()
        first.wait()
        start_rows(1, 0, True, False, False)
        wait_gathers()
        ybuf[1] = jnp.zeros((MOE_ROWS * SLABS, LANES), F32)
        second = idx_copy(1, 0)
        second.start()
        second.wait()

    @pl.when(jnp.logical_or(i == 0, expert != prev_expert))
    def _():
        wgu_bf[...] = wgu_ref[0].astype(BF16)
        wd_bf[...] = wd_ref[0].astype(BF16)

    for parity in (0, 1):
        @pl.when(jnp.logical_and(i < n_used, cur == parity))
        def _(parity=parity):
            idx_copy(i + 2, 1 - parity).start()
            start_rows(parity, 1 - parity, True, True, True)

        @pl.when(jnp.logical_and(i == n_used, cur == parity))
        def _(parity=parity):
            start_rows(parity, 1 - parity, False, True, False)
            wait_scatters()

    @pl.when(i < n_used)
    def _():
        x = jnp.concatenate([xbuf[cur, pl.ds(j, MOE_ROWS, stride=SLABS), :] for j in range(SLABS)],
                            axis=1).astype(BF16)
        y = jnp.broadcast_to(bd_ref[0], (MOE_ROWS, D_MODEL))
        for cidx in range(D_FF // MOE_COLS):
            lo = cidx * MOE_COLS
            gate = jnp.dot(x, wgu_bf[:, lo:lo + MOE_COLS], preferred_element_type=F32) + bgu_ref[0, :, lo:lo + MOE_COLS]
            up = (jnp.dot(x, wgu_bf[:, D_FF + lo:D_FF + lo + MOE_COLS], preferred_element_type=F32)
                  + bgu_ref[0, :, D_FF + lo:D_FF + lo + MOE_COLS])
            gate = jnp.minimum(gate, SWIGLU_LIMIT)
            up = jnp.clip(up, -SWIGLU_LIMIT, SWIGLU_LIMIT)
            act = (up + 1.0) * (gate * jax.nn.sigmoid(SWIGLU_ALPHA * gate))
            y = y + jnp.dot(act.astype(BF16), wd_bf[lo:lo + MOE_COLS, :], preferred_element_type=F32)
        for j in range(SLABS):
            ybuf[cur, pl.ds(j, MOE_ROWS, stride=SLABS), :] = y[:, j * LANES:(j + 1) * LANES]
        wait_gathers()
        wait_scatters()
        idx_copy(0, 0).wait()


def _combine_kernel(gate_ref, x1_ref, mod_ref, gfin_ref, y_ref, o_ref, slab):
    base = pl.program_id(0) * (TOK_TILE * TOP_K)

    def token(t, carry):
        acc = None
        for k in range(TOP_K):
            a = t * TOP_K + k
            term = gate_ref[base + a] * y_ref[_slab_rows(a), :]
            acc = term if acc is None else acc + term
        slab[_slab_rows(t), :] = acc
        return carry

    lax.fori_loop(0, TOK_TILE, token, 0, unroll=8)
    ffn = jnp.concatenate([slab[pl.ds(j, TOK_TILE, stride=SLABS), :] for j in range(SLABS)], axis=1)
    x2 = x1_ref[...] + mod_ref[0, 5:6, :] * ffn
    ms = jnp.mean(x2 * x2, axis=-1, keepdims=True)
    o_ref[...] = x2 * lax.rsqrt(ms + NORM_EPS) * gfin_ref[...]


def _rotary_lane_tables(positions):
    half = ROT_DIM // 2
    inv_freq = ROPE_THETA ** (-jnp.arange(0, ROT_DIM, 2, dtype=F32) / ROT_DIM)
    ang = inv_freq.reshape((half,) + (1,) * positions.ndim) * positions.astype(F32)[None]
    cos, sin = jnp.moveaxis(jnp.cos(ang), 0, -1), jnp.moveaxis(jnp.sin(ang), 0, -1)
    ones = jnp.ones(cos.shape[:-1] + (HEAD_DIM - ROT_DIM,), F32)
    zeros = jnp.zeros_like(ones)
    zero_h = jnp.zeros_like(sin)
    rc = jnp.concatenate([cos, cos, ones], axis=-1)
    rsa = jnp.concatenate([-sin, zero_h, zeros], axis=-1)
    rsb = jnp.concatenate([zero_h, sin, zeros], axis=-1)
    reps = LANES // HEAD_DIM
    return tuple(jnp.tile(t, (1,) * positions.ndim + (reps,)) for t in (rc, rsa, rsb))


def _group_perm(rows):
    run = rows // ATTN_GROUP
    dst = np.arange(rows)
    perm = np.zeros((rows, rows), np.float32)
    perm[dst, ATTN_GROUP * (dst % run) + dst // run] = 1.0
    return jnp.asarray(perm, BF16)


def _band_bias():
    pos = np.arange(ATTN_BLK)
    tables = []
    for d in DILATIONS:
        n_runs = ATTN_GROUP // d
        run = ATTN_BLK // n_runs
        strided = (pos % run) * n_runs + pos // run
        dist = strided[:, None] + ATTN_BLK - np.concatenate([strided, strided + ATTN_BLK])[None, :]
        tables.append(np.where((dist >= 0) & (dist <= ATTN_BLK), 0.0, -np.inf))
    return jnp.asarray(np.stack(tables), F32)


def _attention(q, k, v):
    bsz, n_chunks, _, grp_rows, _ = q.shape

    def chunk_spec(width, back):
        return pl.BlockSpec((1, 1, ATTN_GROUP, grp_rows, width),
                            lambda b, p, ch: (b, jnp.maximum(ch - back, 0), 0, 0, p))
    return pl.pallas_call(
        _attn_kernel,
        grid=(bsz, ATTN_WIDTH // LANES, n_chunks),
        in_specs=[pl.BlockSpec((len(DILATIONS), ATTN_BLK, 2 * ATTN_BLK), lambda b, p, ch: (0, 0, 0)),
                  chunk_spec(LANES, 0), chunk_spec(LANES, 1), chunk_spec(LANES, 0),
                  chunk_spec(LANES, 1), chunk_spec(LANES, 0)],
        out_specs=chunk_spec(LANES, 0),
        out_shape=jax.ShapeDtypeStruct(q.shape, BF16),
        scratch_shapes=[pltpu.VMEM((3, len(DILATIONS), ATTN_GROUP, grp_rows, LANES), F32)],
        compiler_params=pltpu.CompilerParams(dimension_semantics=("arbitrary",) * 3,
                                             vmem_limit_bytes=VMEM_LIMIT),
        name="attn",
    )(_band_bias(), q, k, k, v, v)


def _experts(meta, idx, h2, w_gu, b_gu, w_down, b_down, n_blocks, n_slots):
    d = w_down.shape[-1]

    def expert_block(i, meta):
        return (meta[1 + i], 0, 0)

    any_spec = pl.BlockSpec(memory_space=pl.ANY)
    return pl.pallas_call(
        _expert_kernel,
        grid_spec=pltpu.PrefetchScalarGridSpec(
            num_scalar_prefetch=1,
            grid=(n_blocks + 1,),
            in_specs=[any_spec, any_spec,
                      pl.BlockSpec((1, d, 2 * D_FF), expert_block),
                      pl.BlockSpec((1, 1, 2 * D_FF), expert_block),
                      pl.BlockSpec((1, D_FF, d), expert_block),
                      pl.BlockSpec((1, 1, d), expert_block)],
            out_specs=any_spec,
            scratch_shapes=[pltpu.VMEM((2, MOE_ROWS * SLABS, LANES), F32),
                            pltpu.VMEM((2, MOE_ROWS * SLABS, LANES), F32),
                            pltpu.SMEM((2 * MOE_ROWS,), jnp.int32), pltpu.SMEM((2 * MOE_ROWS,), jnp.int32),
                            pltpu.VMEM((d, 2 * D_FF), BF16), pltpu.VMEM((D_FF, d), BF16),
                            pltpu.SemaphoreType.DMA(()), pltpu.SemaphoreType.DMA(()),
                            pltpu.SemaphoreType.DMA(())]),
        out_shape=jax.ShapeDtypeStruct((n_slots * SLABS, LANES), F32),
        compiler_params=pltpu.CompilerParams(dimension_semantics=("arbitrary",), vmem_limit_bytes=VMEM_LIMIT),
        name="experts",
    )(meta, idx, h2, w_gu, b_gu.reshape(N_EXPERTS, 1, 2 * D_FF), w_down, b_down.reshape(N_EXPERTS, 1, d))


def _routing(top_idx):
    n_tok = top_idx.shape[0]
    n_asg = n_tok * TOP_K
    i32 = jnp.int32
    key = top_idx.reshape(n_asg).astype(i32) * n_asg + jnp.arange(n_asg, dtype=i32)
    skey = lax.sort(key)
    asg_sorted = skey % n_asg
    experts = jnp.arange(N_EXPERTS, dtype=i32)
    counts = jnp.sum((top_idx.reshape(n_asg, 1) == experts[None, :]).astype(i32), axis=0)
    starts = jnp.cumsum(counts) - counts
    padded = (counts + MOE_ROWS - 1) // MOE_ROWS * MOE_ROWS
    pends = jnp.cumsum(padded)
    pstarts = pends - padded
    n_blocks = n_asg // MOE_ROWS + N_EXPERTS
    n_used = (pends[-1] // MOE_ROWS).astype(i32)
    blk = jnp.arange(n_blocks + 2, dtype=i32)
    blk_start = blk * MOE_ROWS
    clamped = jnp.minimum(blk_start, pends[-1] - MOE_ROWS)
    blk_expert = jnp.sum((pends[None, :] <= clamped[:, None]).astype(i32), axis=1)
    row = jnp.arange(MOE_ROWS, dtype=i32)[None, :]
    off = blk_start[:, None] + row - pstarts[blk_expert][:, None]
    valid = jnp.logical_and(off < counts[blk_expert][:, None], blk[:, None] < n_used)
    asg = asg_sorted[jnp.clip(starts[blk_expert][:, None] + off, 0, n_asg - 1)]
    tok = jnp.where(valid, asg // TOP_K, 0)
    spare = jnp.broadcast_to(n_asg + row, (2, MOE_ROWS))
    slot = jnp.concatenate([spare, jnp.where(valid, asg, n_asg + row)[:-2]], axis=0)
    idx = jnp.concatenate([tok, slot], axis=1).reshape(-1).astype(i32)
    meta = jnp.concatenate([n_used[None], blk_expert[:n_blocks + 1]])
    return idx, meta, n_blocks


def kernel(x, c, positions, w_ada, b_ada, g_mix, w_in, conv_w, conv_b, conv_ln_g, conv_ln_b,
           w_out, g_ffn, w_router, b_router, w_gu, b_gu, w_down, b_down, g_final):
    bsz, seq, d = x.shape
    n_tok = bsz * seq
    depth = w_ada.shape[0]
    assert depth == 1, "the final RMSNorm is fused into the layer's combine kernel"
    n_chunks = seq // ATTN_CHUNK
    grp_rows = ATTN_CHUNK // ATTN_GROUP
    tile_run = PROJ_ROWS // ATTN_GROUP
    tiles_per_chunk = ATTN_CHUNK // PROJ_ROWS
    grouped_pos = positions.reshape(bsz, n_chunks, grp_rows, ATTN_GROUP).swapaxes(2, 3)
    rc, rsa, rsb = _rotary_lane_tables(grouped_pos)
    perm = _group_perm(PROJ_ROWS)
    s_tiles = seq // PROJ_ROWS
    arb2 = pltpu.CompilerParams(dimension_semantics=("arbitrary", "arbitrary"), vmem_limit_bytes=VMEM_LIMIT)
    arb1 = pltpu.CompilerParams(dimension_semantics=("arbitrary",), vmem_limit_bytes=VMEM_LIMIT)

    for l in range(depth):
        n_mod = 6 * d
        mod = pl.pallas_call(
            _ada_kernel,
            grid=(n_mod // d,),
            in_specs=[pl.BlockSpec((bsz, d), lambda j: (0, 0)),
                      pl.BlockSpec((d, d), lambda j: (0, j)),
                      pl.BlockSpec((1, d), lambda j: (0, j))],
            out_specs=pl.BlockSpec((bsz, d), lambda j: (0, j)),
            out_shape=jax.ShapeDtypeStruct((bsz, n_mod), F32),
            name="ada",
        )(c, w_ada[l], b_ada[l].reshape(1, n_mod))
        mod = mod.reshape(bsz, 6, d)

        row_spec = lambda width: pl.BlockSpec((1, PROJ_ROWS, width), lambda b, s: (b, s, 0))
        grp_spec = lambda width: pl.BlockSpec(
            (1, 1, ATTN_GROUP, tile_run, width), lambda b, s: (b, s // tiles_per_chunk, 0, s % tiles_per_chunk, 0))
        grp_shape = lambda width, dtype: jax.ShapeDtypeStruct((bsz, n_chunks, ATTN_GROUP, grp_rows, width), dtype)
        full2 = lambda shape: pl.BlockSpec(shape, lambda b, s: (0, 0))
        mod_spec = pl.BlockSpec((1, 6, d), lambda b, s: (b, 0, 0))
        conv_out, q, k, v = pl.pallas_call(
            _proj_kernel,
            grid=(bsz, s_tiles),
            in_specs=[row_spec(d), mod_spec, full2((1, d)), full2((d, IN_COLS)), full2((PROJ_ROWS, PROJ_ROWS)),
                      full2((CONV_WIDTH, CONV_CH)), full2((1, CONV_CH)), full2((1, CONV_CH)), full2((1, CONV_CH)),
                      grp_spec(LANES), grp_spec(LANES), grp_spec(LANES)],
            out_specs=[row_spec(CONV_CH), grp_spec(ATTN_WIDTH), grp_spec(ATTN_WIDTH), grp_spec(ATTN_WIDTH)],
            out_shape=[jax.ShapeDtypeStruct((bsz, seq, CONV_CH), BF16),
                       grp_shape(ATTN_WIDTH, F32), grp_shape(ATTN_WIDTH, F32), grp_shape(ATTN_WIDTH, F32)],
            scratch_shapes=[pltpu.VMEM((PROJ_ROWS + CONV_HALO, CONV_CH), F32)],
            compiler_params=arb2,
            name="proj",
        )(x, mod, g_mix[l].reshape(1, d), w_in[l].astype(BF16), perm, conv_w[l].reshape(CONV_WIDTH, CONV_CH),
          conv_b[l].reshape(1, CONV_CH), conv_ln_g[l].reshape(1, CONV_CH), conv_ln_b[l].reshape(1, CONV_CH),
          rc, rsa, rsb)

        attn_out = _attention(q, k, v)

        tok_rows = lambda width: pl.BlockSpec((PROJ_ROWS, width), lambda b, s: (b * s_tiles + s, 0))
        x1, h2, top_idx, gates = pl.pallas_call(
            _mix_kernel,
            grid=(bsz, s_tiles),
            in_specs=[row_spec(CONV_CH), grp_spec(ATTN_WIDTH), row_spec(d), mod_spec, full2((d, d)),
                      full2((PROJ_ROWS, PROJ_ROWS)), full2((1, d)), full2((d, N_EXPERTS)), full2((1, N_EXPERTS))],
            out_specs=[row_spec(d),
                       pl.BlockSpec((PROJ_ROWS * SLABS, LANES), lambda b, s: (b * s_tiles + s, 0)),
                       tok_rows(TOP_K), tok_rows(TOP_K)],
            out_shape=[jax.ShapeDtypeStruct((bsz, seq, d), F32),
                       jax.ShapeDtypeStruct((n_tok * SLABS, LANES), F32),
                       jax.ShapeDtypeStruct((n_tok, TOP_K), jnp.int32),
                       jax.ShapeDtypeStruct((n_tok, TOP_K), F32)],
            compiler_params=arb2,
            name="mix",
        )(conv_out, attn_out, x, mod, w_out[l].astype(BF16), perm.T, g_ffn[l].reshape(1, d),
          w_router[l].astype(BF16), b_router[l].reshape(1, N_EXPERTS))

        idx, meta, n_blocks = _routing(top_idx)
        n_slots = n_tok * TOP_K + MOE_ROWS
        n_tiles = n_tok // TOK_TILE

        y = _experts(meta, idx, h2, w_gu[l], b_gu[l], w_down[l], b_down[l], n_blocks, n_slots)

        out = pl.pallas_call(
            _combine_kernel,
            grid_spec=pltpu.PrefetchScalarGridSpec(
                num_scalar_prefetch=1,
                grid=(n_tiles,),
                in_specs=[pl.BlockSpec((TOK_TILE, d), lambda i, g: (i, 0)),
                          pl.BlockSpec((1, 6, d), lambda i, g: (i // (seq // TOK_TILE), 0, 0)),
                          pl.BlockSpec((1, d), lambda i, g: (0, 0)),
                          pl.BlockSpec((TOK_TILE * TOP_K * SLABS, LANES), lambda i, g: (i, 0))],
                out_specs=pl.BlockSpec((TOK_TILE, d), lambda i, g: (i, 0)),
                scratch_shapes=[pltpu.VMEM((TOK_TILE * SLABS, LANES), F32)]),
            out_shape=jax.ShapeDtypeStruct((n_tok, d), F32),
            compiler_params=arb1,
            name="combine",
        )(gates.reshape(-1), x1.reshape(n_tok, d), mod, g_final.reshape(1, d), y)
        x = out.reshape(bsz, seq, d)
    return x
```

```python
import functools

import jax
import numpy as np
import jax.numpy as jnp
from jax import lax
from jax.experimental import pallas as pl
from jax.experimental.pallas import tpu as pltpu

F32 = jnp.float32
BF16 = jnp.bfloat16

D_MODEL = 1024
N_HEADS = 8
HEAD_DIM = 64
ATTN_WIDTH = N_HEADS * HEAD_DIM
CONV_CH = D_MODEL - ATTN_WIDTH
IN_COLS = 2 * CONV_CH + 3 * ATTN_WIDTH
CONV_WIDTH = 31
ROT_DIM = HEAD_DIM // 4
ROPE_THETA = 500000.0
DILATIONS = (1, 4, 16)
ATTN_BLK = 128
ATTN_GROUP = 16
N_EXPERTS = 32
TOP_K = 4
D_FF = D_MODEL
SWIGLU_LIMIT = 7.0
SWIGLU_ALPHA = 1.702
NORM_EPS = 1e-6

LANES = 128
SUBLANES = 8
SLABS = D_MODEL // LANES
VMEM_LIMIT = 56 * 1024 * 1024

PROJ_ROWS = 512
CONV_HALO = 32
CONV_ROWS = 64
ATTN_CHUNK = 2048
MOE_ROWS = 512
MOE_COLS = 512
TOK_TILE = 256


def _ada_kernel(c_ref, w_ref, b_ref, o_ref):
    c = c_ref[...]
    c_act = c * jax.nn.sigmoid(c)
    o_ref[...] = jnp.dot(c_act.astype(BF16), w_ref[...].astype(BF16), preferred_element_type=F32) + b_ref[...]


def _rms_modulate(x, gain, shift, scale):
    ms = jnp.mean(x * x, axis=-1, keepdims=True)
    y = x * lax.rsqrt(ms + NORM_EPS) * gain
    return y * (1.0 + scale) + shift


def _proj_kernel(x_ref, mod_ref, gmix_ref, win_ref, perm_ref, cw_ref, cb_ref, lng_ref, lnb_ref,
                 rc_ref, rsa_ref, rsb_ref, co_ref, q_ref, k_ref, v_ref, ubuf):
    rows = x_ref.shape[1]
    hb = _rms_modulate(x_ref[0], gmix_ref[...], mod_ref[0, 0:1, :], mod_ref[0, 1:2, :]).astype(BF16)

    ag = jnp.dot(hb, win_ref[:, :2 * CONV_CH], preferred_element_type=F32)
    u = ag[:, :CONV_CH] * jax.nn.sigmoid(ag[:, CONV_CH:])

    @pl.when(pl.program_id(1) == 0)
    def _():
        ubuf[0:CONV_HALO, :] = jnp.zeros((CONV_HALO, CONV_CH), F32)

    ubuf[CONV_HALO:, :] = u
    first_tap = CONV_HALO - (CONV_WIDTH - 1)
    for ch in range(rows // CONV_ROWS):
        row0 = ch * CONV_ROWS
        acc = jnp.broadcast_to(cb_ref[...], (CONV_ROWS, CONV_CH))
        for shift in range(SUBLANES):
            taps = [j for j in range(CONV_WIDTH) if (first_tap + j) % SUBLANES == shift]
            span = CONV_ROWS + (SUBLANES if shift else 0)
            part = None
            for j in taps:
                base = row0 + first_tap + j - shift
                term = cw_ref[j:j + 1, :] * ubuf[base:base + span, :]
                part = term if part is None else part + term
            acc = acc + part[shift:shift + CONV_ROWS]
        mu = jnp.mean(acc, axis=-1, keepdims=True)
        cen = acc - mu
        var = jnp.mean(cen * cen, axis=-1, keepdims=True)
        yn = cen * lax.rsqrt(var + NORM_EPS) * lng_ref[...] + lnb_ref[...]
        co_ref[0, ch * CONV_ROWS:(ch + 1) * CONV_ROWS, :] = (yn * jax.nn.sigmoid(yn)).astype(co_ref.dtype)
    ubuf[0:CONV_HALO, :] = ubuf[rows:rows + CONV_HALO, :]

    hp = jnp.dot(perm_ref[...], hb, preferred_element_type=F32).astype(BF16)
    qkv = jnp.dot(hp, win_ref[:, 2 * CONV_CH:], preferred_element_type=F32)
    run = rows // ATTN_GROUP
    tile = (ATTN_GROUP, run, LANES)
    rc, rsa, rsb = (t[0, 0].reshape(rows, LANES) for t in (rc_ref, rsa_ref, rsb_ref))
    half = ROT_DIM // 2
    for blk in range(ATTN_WIDTH // LANES):
        lo, hi = blk * LANES, (blk + 1) * LANES
        for src0, dst_ref, scale in ((0, q_ref, HEAD_DIM ** -0.5), (ATTN_WIDTH, k_ref, 1.0)):
            t = qkv[:, src0 + lo:src0 + hi]
            rot = t * rc + pltpu.roll(t, LANES - half, 1) * rsa + pltpu.roll(t, half, 1) * rsb
            dst_ref[0, 0, :, :, lo:hi] = (rot * scale).reshape(tile)
        v_ref[0, 0, :, :, lo:hi] = qkv[:, 2 * ATTN_WIDTH + lo:2 * ATTN_WIDTH + hi].reshape(tile)


def _attn_kernel(bias_ref, q_ref, kp_ref, kc_ref, vp_ref, vc_ref, o_ref, st):
    chunk = pl.program_id(2)
    lane = lax.broadcasted_iota(jnp.int32, (ATTN_BLK, LANES), 1)
    head0 = lane < HEAD_DIM
    kcol = lax.broadcasted_iota(jnp.int32, (ATTN_BLK, 2 * ATTN_BLK), 1)
    neg_inf = jnp.float32(-jnp.inf)
    n_blocks = ATTN_CHUNK // ATTN_BLK
    all_lanes = slice(None)

    for pi, d in enumerate(DILATIONS):
        n_runs = ATTN_GROUP // d
        run = ATTN_BLK // n_runs
        n_sub = n_blocks // d
        bias = bias_ref[pi]
        bias_first = jnp.where(jnp.logical_or(kcol >= ATTN_BLK, chunk > 0), bias, neg_inf)
        for idx in range(n_blocks):
            u, res = idx // d, idx % d
            a_cur = u * run
            prev_k, prev_v, a_prev = (kc_ref, vc_ref, a_cur - run) if u > 0 else (kp_ref, vp_ref, (n_sub - 1) * run)

            def rows_of(ref, a0, lanes):
                parts = [ref[0, 0, i * d + res, a0:a0 + run, lanes] for i in range(n_runs)]
                return parts[0] if n_runs == 1 else jnp.concatenate(parts, axis=0)

            qb = rows_of(q_ref, a_cur, all_lanes)
            kb = jnp.concatenate([rows_of(prev_k, a_prev, all_lanes),
                                  rows_of(kc_ref, a_cur, all_lanes)], axis=0).astype(BF16)
            q2 = jnp.concatenate([jnp.where(head0, qb, 0.0), jnp.where(head0, 0.0, qb)], axis=0).astype(BF16)
            s2 = lax.dot_general(q2, kb, (((1,), (1,)), ((), ())), preferred_element_type=F32)
            vb = jnp.concatenate([rows_of(prev_v, a_prev, all_lanes),
                                  rows_of(vc_ref, a_cur, all_lanes)], axis=0).astype(BF16)
            parts = []
            for hsel in range(2):
                s = s2[hsel * ATTN_BLK:(hsel + 1) * ATTN_BLK] + (bias if u > 0 else bias_first)
                m = jnp.max(s, axis=1, keepdims=True)
                p = jnp.exp(s - m)
                parts.append((jnp.dot(p.astype(BF16), vb, preferred_element_type=F32),
                              m, jnp.sum(p, axis=1, keepdims=True)))
            vals = [jnp.where(head0, a, b) for a, b in zip(*parts)]
            for i in range(n_runs):
                for si, val in enumerate(vals):
                    st[si, pi, i * d + res, a_cur:a_cur + run, :] = val[i * run:(i + 1) * run]

    def merge(g, carry):
        maxes = [st[1, pi, g] for pi in range(len(DILATIONS))]
        top = functools.reduce(jnp.maximum, maxes)
        weights = [jnp.exp(mx - top) for mx in maxes]
        num = sum(w * st[0, pi, g] for pi, w in enumerate(weights))
        den = sum(w * st[2, pi, g] for pi, w in enumerate(weights))
        o_ref[0, 0, g] = (num / den).astype(o_ref.dtype)
        return carry

    lax.fori_loop(0, ATTN_GROUP, merge, 0, unroll=2)


def _top_k_gates(logits):
    expert = lax.broadcasted_iota(jnp.int32, logits.shape, 0).astype(F32)
    vals, idxs = [], []
    rest = logits
    for _ in range(TOP_K):
        best = jnp.max(rest, axis=0, keepdims=True)
        where_best = jnp.min(jnp.where(rest == best, expert, float(N_EXPERTS)), axis=0, keepdims=True)
        vals.append(best)
        idxs.append(where_best)
        rest = jnp.where(expert == where_best, -jnp.inf, rest)
    weights = [jnp.exp(v - vals[0]) for v in vals]
    total = functools.reduce(lambda a, b: a + b, weights)
    gates = jnp.concatenate([w / total for w in weights], axis=0)
    return jnp.concatenate(idxs, axis=0).astype(jnp.int32), gates


def _mix_kernel(co_ref, ao_ref, x_ref, mod_ref, wo_ref, unperm_ref, gffn_ref, wr_ref, br_ref,
                x1_ref, h2_ref, idx_ref, gate_ref):
    rows = x_ref.shape[1]
    ao = jnp.dot(unperm_ref[...], ao_ref[0, 0].reshape(rows, ATTN_WIDTH),
                 preferred_element_type=F32).astype(BF16)
    mixed = (jnp.dot(co_ref[0], wo_ref[0:CONV_CH, :], preferred_element_type=F32)
             + jnp.dot(ao, wo_ref[CONV_CH:, :], preferred_element_type=F32))
    x1 = x_ref[0] + mod_ref[0, 2:3, :] * mixed
    x1_ref[0] = x1
    h2 = _rms_modulate(x1, gffn_ref[...], mod_ref[0, 3:4, :], mod_ref[0, 4:5, :])
    for j in range(SLABS):
        h2_ref[pl.ds(j, rows, stride=SLABS), :] = h2[:, j * LANES:(j + 1) * LANES]
    logits = lax.dot_general(wr_ref[...], h2.astype(BF16), (((1,), (1,)), ((), ())),
                             preferred_element_type=F32) + br_ref[...]
    idx_ref[...], gate_ref[...] = _top_k_gates(logits)


def _slab_rows(idx):
    return pl.ds(pl.multiple_of(idx * SLABS, SLABS), SLABS)


def _expert_kernel(meta_ref, idx_hbm, h2_hbm, wgu_ref, bgu_ref, wd_ref, bd_ref, y_hbm,
                   xbuf, ybuf, idx_a, idx_b, wgu_bf, wd_bf, gsem, ssem, isem):
    i = pl.program_id(0)
    n_used = meta_ref[0]
    expert = meta_ref[1 + i]
    prev_expert = meta_ref[jnp.maximum(i, 1)]
    cur = i % 2
    idx_bufs = (idx_a, idx_b)
    idx_row = 2 * MOE_ROWS

    def idx_copy(row, slot):
        start = pl.multiple_of(row * idx_row, idx_row)
        return pltpu.make_async_copy(idx_hbm.at[pl.ds(start, idx_row)], idx_bufs[slot], isem)

    def row_slab(r):
        return pl.ds(r * SLABS, SLABS) if isinstance(r, int) else _slab_rows(r)

    def gather_copy(idx_slot, buf_slot, r):
        return pltpu.make_async_copy(h2_hbm.at[_slab_rows(idx_bufs[idx_slot][r]), :],
                                     xbuf.at[buf_slot, row_slab(r), :], gsem)

    def scatter_copy(idx_slot, buf_slot, r):
        return pltpu.make_async_copy(ybuf.at[buf_slot, row_slab(r), :],
                                     y_hbm.at[_slab_rows(idx_bufs[idx_slot][MOE_ROWS + r]), :], ssem)

    def start_rows(idx_slot, buf_slot, gather, scatter, unrolled):
        def body(r, carry):
            if gather:
                gather_copy(idx_slot, buf_slot, r).start()
            if scatter:
                scatter_copy(idx_slot, buf_slot, r).start()
            return carry
        if unrolled:
            for r in range(MOE_ROWS):
                body(r, 0)
        else:
            lax.fori_loop(0, MOE_ROWS, body, 0, unroll=8)

    def wait_gathers():
        pltpu.make_async_copy(h2_hbm.at[pl.ds(0, MOE_ROWS * SLABS), :], xbuf.at[0], gsem).wait()

    def wait_scatters():
        pltpu.make_async_copy(ybuf.at[0], y_hbm.at[pl.ds(0, MOE_ROWS * SLABS), :], ssem).wait()

    @pl.when(i == 0)
    def _():
        first = idx_copy(0, 1)
        first.start()
        first.wait()
        start_rows(1, 0, True, False, False)
        wait_gathers()
        ybuf[1] = jnp.zeros((MOE_ROWS * SLABS, LANES), F32)
        second = idx_copy(1, 0)
        second.start()
        second.wait()

    @pl.when(jnp.logical_or(i == 0, expert != prev_expert))
    def _():
        wgu_bf[...] = wgu_ref[0].astype(BF16)
        wd_bf[...] = wd_ref[0].astype(BF16)

    for parity in (0, 1):
        @pl.when(jnp.logical_and(i < n_used, cur == parity))
        def _(parity=parity):
            idx_copy(i + 2, 1 - parity).start()
            start_rows(parity, 1 - parity, True, True, True)

        @pl.when(jnp.logical_and(i == n_used, cur == parity))
        def _(parity=parity):
            start_rows(parity, 1 - parity, False, True, False)
            wait_scatters()

    @pl.when(i < n_used)
    def _():
        x = jnp.concatenate([xbuf[cur, pl.ds(j, MOE_ROWS, stride=SLABS), :] for j in range(SLABS)],
                            axis=1).astype(BF16)
        y = jnp.broadcast_to(bd_ref[0], (MOE_ROWS, D_MODEL))
        for cidx in range(D_FF // MOE_COLS):
            lo = cidx * MOE_COLS
            gate = jnp.dot(x, wgu_bf[:, lo:lo + MOE_COLS], preferred_element_type=F32) + bgu_ref[0, :, lo:lo + MOE_COLS]
            up = (jnp.dot(x, wgu_bf[:, D_FF + lo:D_FF + lo + MOE_COLS], preferred_element_type=F32)
                  + bgu_ref[0, :, D_FF + lo:D_FF + lo + MOE_COLS])
            gate = jnp.minimum(gate, SWIGLU_LIMIT)
            up = jnp.clip(up, -SWIGLU_LIMIT, SWIGLU_LIMIT)
            act = (up + 1.0) * (gate * jax.nn.sigmoid(SWIGLU_ALPHA * gate))
            y = y + jnp.dot(act.astype(BF16), wd_bf[lo:lo + MOE_COLS, :], preferred_element_type=F32)
        for j in range(SLABS):
            ybuf[cur, pl.ds(j, MOE_ROWS, stride=SLABS), :] = y[:, j * LANES:(j + 1) * LANES]
        wait_gathers()
        wait_scatters()
        idx_copy(0, 0).wait()


def _combine_kernel(gate_ref, x1_ref, mod_ref, gfin_ref, y_ref, o_ref, slab):
    base = pl.program_id(0) * (TOK_TILE * TOP_K)

    def token(t, carry):
        acc = None
        for k in range(TOP_K):
            a = t * TOP_K + k
            term = gate_ref[base + a] * y_ref[_slab_rows(a), :]
            acc = term if acc is None else acc + term
        slab[_slab_rows(t), :] = acc
        return carry

    lax.fori_loop(0, TOK_TILE, token, 0, unroll=8)
    ffn = jnp.concatenate([slab[pl.ds(j, TOK_TILE, stride=SLABS), :] for j in range(SLABS)], axis=1)
    x2 = x1_ref[...] + mod_ref[0, 5:6, :] * ffn
    ms = jnp.mean(x2 * x2, axis=-1, keepdims=True)
    o_ref[...] = x2 * lax.rsqrt(ms + NORM_EPS) * gfin_ref[...]


def _rotary_lane_tables(positions):
    half = ROT_DIM // 2
    inv_freq = ROPE_THETA ** (-jnp.arange(0, ROT_DIM, 2, dtype=F32) / ROT_DIM)
    ang = inv_freq.reshape((half,) + (1,) * positions.ndim) * positions.astype(F32)[None]
    cos, sin = lax.optimization_barrier((jnp.cos(ang), jnp.sin(ang)))
    cos, sin = jnp.moveaxis(cos, 0, -1), jnp.moveaxis(sin, 0, -1)
    ones = jnp.ones(cos.shape[:-1] + (HEAD_DIM - ROT_DIM,), F32)
    zeros = jnp.zeros_like(ones)
    zero_h = jnp.zeros_like(sin)
    rc = jnp.concatenate([cos, cos, ones], axis=-1)
    rsa = jnp.concatenate([-sin, zero_h, zeros], axis=-1)
    rsb = jnp.concatenate([zero_h, sin, zeros], axis=-1)
    reps = LANES // HEAD_DIM
    return tuple(jnp.tile(t, (1,) * positions.ndim + (reps,)) for t in (rc, rsa, rsb))


def _group_perm(rows):
    run = rows // ATTN_GROUP
    dst = np.arange(rows)
    perm = np.zeros((rows, rows), np.float32)
    perm[dst, ATTN_GROUP * (dst % run) + dst // run] = 1.0
    return jnp.asarray(perm, BF16)


def _band_bias():
    pos = np.arange(ATTN_BLK)
    tables = []
    for d in DILATIONS:
        n_runs = ATTN_GROUP // d
        run = ATTN_BLK // n_runs
        strided = (pos % run) * n_runs + pos // run
        dist = strided[:, None] + ATTN_BLK - np.concatenate([strided, strided + ATTN_BLK])[None, :]
        tables.append(np.where((dist >= 0) & (dist <= ATTN_BLK), 0.0, -np.inf))
    return jnp.asarray(np.stack(tables), F32)


def _attention(q, k, v):
    bsz, n_chunks, _, grp_rows, _ = q.shape

    def chunk_spec(width, back):
        return pl.BlockSpec((1, 1, ATTN_GROUP, grp_rows, width),
                            lambda b, p, ch: (b, jnp.maximum(ch - back, 0), 0, 0, p))
    return pl.pallas_call(
        _attn_kernel,
        grid=(bsz, ATTN_WIDTH // LANES, n_chunks),
        in_specs=[pl.BlockSpec((len(DILATIONS), ATTN_BLK, 2 * ATTN_BLK), lambda b, p, ch: (0, 0, 0)),
                  chunk_spec(LANES, 0), chunk_spec(LANES, 1), chunk_spec(LANES, 0),
                  chunk_spec(LANES, 1), chunk_spec(LANES, 0)],
        out_specs=chunk_spec(LANES, 0),
        out_shape=jax.ShapeDtypeStruct(q.shape, BF16),
        scratch_shapes=[pltpu.VMEM((3, len(DILATIONS), ATTN_GROUP, grp_rows, LANES), F32)],
        compiler_params=pltpu.CompilerParams(dimension_semantics=("arbitrary",) * 3,
                                             vmem_limit_bytes=VMEM_LIMIT),
        name="attn",
    )(_band_bias(), q, k, k, v, v)


def _experts(meta, idx, h2, w_gu, b_gu, w_down, b_down, n_blocks, n_slots):
    d = w_down.shape[-1]

    def expert_block(i, meta):
        return (meta[1 + i], 0, 0)

    any_spec = pl.BlockSpec(memory_space=pl.ANY)
    return pl.pallas_call(
        _expert_kernel,
        grid_spec=pltpu.PrefetchScalarGridSpec(
            num_scalar_prefetch=1,
            grid=(n_blocks + 1,),
            in_specs=[any_spec, any_spec,
                      pl.BlockSpec((1, d, 2 * D_FF), expert_block),
                      pl.BlockSpec((1, 1, 2 * D_FF), expert_block),
                      pl.BlockSpec((1, D_FF, d), expert_block),
                      pl.BlockSpec((1, 1, d), expert_block)],
            out_specs=any_spec,
            scratch_shapes=[pltpu.VMEM((2, MOE_ROWS * SLABS, LANES), F32),
                            pltpu.VMEM((2, MOE_ROWS * SLABS, LANES), F32),
                            pltpu.SMEM((2 * MOE_ROWS,), jnp.int32), pltpu.SMEM((2 * MOE_ROWS,), jnp.int32),
                            pltpu.VMEM((d, 2 * D_FF), BF16), pltpu.VMEM((D_FF, d), BF16),
                            pltpu.SemaphoreType.DMA(()), pltpu.SemaphoreType.DMA(()),
                            pltpu.SemaphoreType.DMA(())]),
        out_shape=jax.ShapeDtypeStruct((n_slots * SLABS, LANES), F32),
        compiler_params=pltpu.CompilerParams(dimension_semantics=("arbitrary",), vmem_limit_bytes=VMEM_LIMIT),
        name="experts",
    )(meta, idx, h2, w_gu, b_gu.reshape(N_EXPERTS, 1, 2 * D_FF), w_down, b_down.reshape(N_EXPERTS, 1, d))


def _routing(top_idx):
    n_tok = top_idx.shape[1]
    n_asg = n_tok * TOP_K
    i32 = jnp.int32
    slot_of = jnp.arange(n_tok, dtype=i32)[None, :] * TOP_K + jnp.arange(TOP_K, dtype=i32)[:, None]
    key = (top_idx.astype(i32) * n_asg + slot_of).reshape(n_asg)
    skey = lax.sort(key)
    asg_sorted = skey % n_asg
    experts = jnp.arange(N_EXPERTS, dtype=i32)
    counts = jnp.sum((top_idx.reshape(1, n_asg) == experts[:, None]).astype(i32), axis=1)
    starts = jnp.cumsum(counts) - counts
    padded = (counts + MOE_ROWS - 1) // MOE_ROWS * MOE_ROWS
    pends = jnp.cumsum(padded)
    pstarts = pends - padded
    n_blocks = n_asg // MOE_ROWS + N_EXPERTS
    n_used = (pends[-1] // MOE_ROWS).astype(i32)
    blk = jnp.arange(n_blocks + 2, dtype=i32)
    blk_start = blk * MOE_ROWS
    clamped = jnp.minimum(blk_start, pends[-1] - MOE_ROWS)
    blk_expert = jnp.sum((pends[None, :] <= clamped[:, None]).astype(i32), axis=1)
    row = jnp.arange(MOE_ROWS, dtype=i32)[None, :]
    off = blk_start[:, None] + row - pstarts[blk_expert][:, None]
    valid = jnp.logical_and(off < counts[blk_expert][:, None], blk[:, None] < n_used)
    asg = asg_sorted[jnp.clip(starts[blk_expert][:, None] + off, 0, n_asg - 1)]
    tok = jnp.where(valid, asg // TOP_K, 0)
    spare = jnp.broadcast_to(n_asg + row, (2, MOE_ROWS))
    slot = jnp.concatenate([spare, jnp.where(valid, asg, n_asg + row)[:-2]], axis=0)
    idx = jnp.concatenate([tok, slot], axis=1).reshape(-1).astype(i32)
    meta = jnp.concatenate([n_used[None], blk_expert[:n_blocks + 1]])
    return idx, meta, n_blocks


def kernel(x, c, positions, w_ada, b_ada, g_mix, w_in, conv_w, conv_b, conv_ln_g, conv_ln_b,
           w_out, g_ffn, w_router, b_router, w_gu, b_gu, w_down, b_down, g_final):
    bsz, seq, d = x.shape
    n_tok = bsz * seq
    depth = w_ada.shape[0]
    assert depth == 1, "the final RMSNorm is fused into the layer's combine kernel"
    n_chunks = seq // ATTN_CHUNK
    grp_rows = ATTN_CHUNK // ATTN_GROUP
    tile_run = PROJ_ROWS // ATTN_GROUP
    tiles_per_chunk = ATTN_CHUNK // PROJ_ROWS
    grouped_pos = positions.reshape(bsz, n_chunks, grp_rows, ATTN_GROUP).swapaxes(2, 3)
    rc, rsa, rsb = _rotary_lane_tables(grouped_pos)
    perm = _group_perm(PROJ_ROWS)
    s_tiles = seq // PROJ_ROWS
    arb2 = pltpu.CompilerParams(dimension_semantics=("arbitrary", "arbitrary"), vmem_limit_bytes=VMEM_LIMIT)
    arb1 = pltpu.CompilerParams(dimension_semantics=("arbitrary",), vmem_limit_bytes=VMEM_LIMIT)

    for l in range(depth):
        n_mod = 6 * d
        mod = pl.pallas_call(
            _ada_kernel,
            grid=(n_mod // d,),
            in_specs=[pl.BlockSpec((bsz, d), lambda j: (0, 0)),
                      pl.BlockSpec((d, d), lambda j: (0, j)),
                      pl.BlockSpec((1, d), lambda j: (0, j))],
            out_specs=pl.BlockSpec((bsz, d), lambda j: (0, j)),
            out_shape=jax.ShapeDtypeStruct((bsz, n_mod), F32),
            name="ada",
        )(c, w_ada[l], b_ada[l].reshape(1, n_mod))
        mod = mod.reshape(bsz, 6, d)

        row_spec = lambda width: pl.BlockSpec((1, PROJ_ROWS, width), lambda b, s: (b, s, 0))
        grp_spec = lambda width: pl.BlockSpec(
            (1, 1, ATTN_GROUP, tile_run, width), lambda b, s: (b, s // tiles_per_chunk, 0, s % tiles_per_chunk, 0))
        grp_shape = lambda width, dtype: jax.ShapeDtypeStruct((bsz, n_chunks, ATTN_GROUP, grp_rows, width), dtype)
        full2 = lambda shape: pl.BlockSpec(shape, lambda b, s: (0, 0))
        mod_spec = pl.BlockSpec((1, 6, d), lambda b, s: (b, 0, 0))
        conv_out, q, k, v = pl.pallas_call(
            _proj_kernel,
            grid=(bsz, s_tiles),
            in_specs=[row_spec(d), mod_spec, full2((1, d)), full2((d, IN_COLS)), full2((PROJ_ROWS, PROJ_ROWS)),
                      full2((CONV_WIDTH, CONV_CH)), full2((1, CONV_CH)), full2((1, CONV_CH)), full2((1, CONV_CH)),
                      grp_spec(LANES), grp_spec(LANES), grp_spec(LANES)],
            out_specs=[row_spec(CONV_CH), grp_spec(ATTN_WIDTH), grp_spec(ATTN_WIDTH), grp_spec(ATTN_WIDTH)],
            out_shape=[jax.ShapeDtypeStruct((bsz, seq, CONV_CH), BF16),
                       grp_shape(ATTN_WIDTH, F32), grp_shape(ATTN_WIDTH, F32), grp_shape(ATTN_WIDTH, F32)],
            scratch_shapes=[pltpu.VMEM((PROJ_ROWS + CONV_HALO, CONV_CH), F32)],
            compiler_params=arb2,
            name="proj",
        )(x, mod, g_mix[l].reshape(1, d), w_in[l].astype(BF16), perm, conv_w[l].reshape(CONV_WIDTH, CONV_CH),
          conv_b[l].reshape(1, CONV_CH), conv_ln_g[l].reshape(1, CONV_CH), conv_ln_b[l].reshape(1, CONV_CH),
          rc, rsa, rsb)

        attn_out = _attention(q, k, v)

        tok_cols = pl.BlockSpec((TOP_K, PROJ_ROWS), lambda b, s: (0, b * s_tiles + s))
        x1, h2, top_idx, gates = pl.pallas_call(
            _mix_kernel,
            grid=(bsz, s_tiles),
            in_specs=[row_spec(CONV_CH), grp_spec(ATTN_WIDTH), row_spec(d), mod_spec, full2((d, d)),
                      full2((PROJ_ROWS, PROJ_ROWS)), full2((1, d)), full2((N_EXPERTS, d)), full2((N_EXPERTS, 1))],
            out_specs=[row_spec(d),
                       pl.BlockSpec((PROJ_ROWS * SLABS, LANES), lambda b, s: (b * s_tiles + s, 0)),
                       tok_cols, tok_cols],
            out_shape=[jax.ShapeDtypeStruct((bsz, seq, d), F32),
                       jax.ShapeDtypeStruct((n_tok * SLABS, LANES), F32),
                       jax.ShapeDtypeStruct((TOP_K, n_tok), jnp.int32),
                       jax.ShapeDtypeStruct((TOP_K, n_tok), F32)],
            compiler_params=arb2,
            name="mix",
        )(conv_out, attn_out, x, mod, w_out[l].astype(BF16), perm.T, g_ffn[l].reshape(1, d),
          w_router[l].T.astype(BF16), b_router[l].reshape(N_EXPERTS, 1))
        gates = gates.T

        idx, meta, n_blocks = _routing(top_idx)
        n_slots = n_tok * TOP_K + MOE_ROWS
        n_tiles = n_tok // TOK_TILE

        y = _experts(meta, idx, h2, w_gu[l], b_gu[l], w_down[l], b_down[l], n_blocks, n_slots)

        out = pl.pallas_call(
            _combine_kernel,
            grid_spec=pltpu.PrefetchScalarGridSpec(
                num_scalar_prefetch=1,
                grid=(n_tiles,),
                in_specs=[pl.BlockSpec((TOK_TILE, d), lambda i, g: (i, 0)),
                          pl.BlockSpec((1, 6, d), lambda i, g: (i // (seq // TOK_TILE), 0, 0)),
                          pl.BlockSpec((1, d), lambda i, g: (0, 0)),
                          pl.BlockSpec((TOK_TILE * TOP_K * SLABS, LANES), lambda i, g: (i, 0))],
                out_specs=pl.BlockSpec((TOK_TILE, d), lambda i, g: (i, 0)),
                scratch_shapes=[pltpu.VMEM((TOK_TILE * SLABS, LANES), F32)]),
            out_shape=jax.ShapeDtypeStruct((n_tok, d), F32),
            compiler_params=arb1,
            name="combine",
        )(gates.reshape(-1), x1.reshape(n_tok, d), mod, g_final.reshape(1, d), y)
        x = out.reshape(bsz, seq, d)
    return x
```

```python
import functools

import jax
import numpy as np
import jax.numpy as jnp
from jax import lax
from jax.experimental import pallas as pl
from jax.experimental.pallas import tpu as pltpu

F32 = jnp.float32
BF16 = jnp.bfloat16

D_MODEL = 1024
N_HEADS = 8
HEAD_DIM = 64
ATTN_WIDTH = N_HEADS * HEAD_DIM
CONV_CH = D_MODEL - ATTN_WIDTH
IN_COLS = 2 * CONV_CH + 3 * ATTN_WIDTH
CONV_WIDTH = 31
ROT_DIM = HEAD_DIM // 4
ROPE_THETA = 500000.0
DILATIONS = (1, 4, 16)
ATTN_BLK = 128
ATTN_GROUP = 16
N_EXPERTS = 32
TOP_K = 4
D_FF = D_MODEL
SWIGLU_LIMIT = 7.0
SWIGLU_ALPHA = 1.702
NORM_EPS = 1e-6

LANES = 128
SUBLANES = 8
SLABS = D_MODEL // LANES
VMEM_LIMIT = 56 * 1024 * 1024

PROJ_ROWS = 512
CONV_HALO = 32
CONV_ROWS = 64
ATTN_CHUNK = 2048
MOE_ROWS = 512
MOE_COLS = 512
TOK_TILE = 256


def _ada_kernel(c_ref, w_ref, b_ref, o_ref):
    c = c_ref[...]
    c_act = c * jax.nn.sigmoid(c)
    o_ref[...] = jnp.dot(c_act.astype(BF16), w_ref[...].astype(BF16), preferred_element_type=F32) + b_ref[...]


def _rms_modulate(x, gain, shift, scale):
    ms = jnp.mean(x * x, axis=-1, keepdims=True)
    y = x * lax.rsqrt(ms + NORM_EPS) * gain
    return y * (1.0 + scale) + shift


def _proj_kernel(x_ref, mod_ref, gmix_ref, win_ref, perm_ref, cw_ref, cb_ref, lng_ref, lnb_ref,
                 cs_ref, rexp_ref, roff_ref, co_ref, q_ref, k_ref, v_ref, ubuf):
    rows = x_ref.shape[1]
    hb = _rms_modulate(x_ref[0], gmix_ref[...], mod_ref[0, 0:1, :], mod_ref[0, 1:2, :]).astype(BF16)

    ag = jnp.dot(hb, win_ref[:, :2 * CONV_CH], preferred_element_type=F32)
    u = ag[:, :CONV_CH] * jax.nn.sigmoid(ag[:, CONV_CH:])

    @pl.when(pl.program_id(1) == 0)
    def _():
        ubuf[0:CONV_HALO, :] = jnp.zeros((CONV_HALO, CONV_CH), F32)

    ubuf[CONV_HALO:, :] = u
    first_tap = CONV_HALO - (CONV_WIDTH - 1)
    for ch in range(rows // CONV_ROWS):
        row0 = ch * CONV_ROWS
        acc = jnp.broadcast_to(cb_ref[...], (CONV_ROWS, CONV_CH))
        for shift in range(SUBLANES):
            taps = [j for j in range(CONV_WIDTH) if (first_tap + j) % SUBLANES == shift]
            span = CONV_ROWS + (SUBLANES if shift else 0)
            part = None
            for j in taps:
                base = row0 + first_tap + j - shift
                term = cw_ref[j:j + 1, :] * ubuf[base:base + span, :]
                part = term if part is None else part + term
            acc = acc + part[shift:shift + CONV_ROWS]
        mu = jnp.mean(acc, axis=-1, keepdims=True)
        cen = acc - mu
        var = jnp.mean(cen * cen, axis=-1, keepdims=True)
        yn = cen * lax.rsqrt(var + NORM_EPS) * lng_ref[...] + lnb_ref[...]
        co_ref[0, ch * CONV_ROWS:(ch + 1) * CONV_ROWS, :] = (yn * jax.nn.sigmoid(yn)).astype(co_ref.dtype)
    ubuf[0:CONV_HALO, :] = ubuf[rows:rows + CONV_HALO, :]

    hp = jnp.dot(perm_ref[...], hb, preferred_element_type=F32).astype(BF16)
    qkv = jnp.dot(hp, win_ref[:, 2 * CONV_CH:], preferred_element_type=F32)
    run = rows // ATTN_GROUP
    tile = (ATTN_GROUP, run, LANES)
    tables = jnp.dot(cs_ref[0, 0].reshape(rows, ROT_DIM), rexp_ref[...], preferred_element_type=F32,
                     precision=lax.Precision.HIGHEST) + roff_ref[...]
    rc, rsa, rsb = (tables[:, i * LANES:(i + 1) * LANES] for i in range(3))
    half = ROT_DIM // 2
    for blk in range(ATTN_WIDTH // LANES):
        lo, hi = blk * LANES, (blk + 1) * LANES
        for src0, dst_ref, scale in ((0, q_ref, HEAD_DIM ** -0.5), (ATTN_WIDTH, k_ref, 1.0)):
            t = qkv[:, src0 + lo:src0 + hi]
            rot = t * rc + pltpu.roll(t, LANES - half, 1) * rsa + pltpu.roll(t, half, 1) * rsb
            dst_ref[0, 0, :, :, lo:hi] = (rot * scale).reshape(tile)
        v_ref[0, 0, :, :, lo:hi] = qkv[:, 2 * ATTN_WIDTH + lo:2 * ATTN_WIDTH + hi].reshape(tile)


def _attn_kernel(bias_ref, q_ref, kp_ref, kc_ref, vp_ref, vc_ref, o_ref, st):
    chunk = pl.program_id(2)
    lane = lax.broadcasted_iota(jnp.int32, (ATTN_BLK, LANES), 1)
    head0 = lane < HEAD_DIM
    kcol = lax.broadcasted_iota(jnp.int32, (ATTN_BLK, 2 * ATTN_BLK), 1)
    neg_inf = jnp.float32(-jnp.inf)
    n_blocks = ATTN_CHUNK // ATTN_BLK
    all_lanes = slice(None)

    for pi, d in enumerate(DILATIONS):
        n_runs = ATTN_GROUP // d
        run = ATTN_BLK // n_runs
        n_sub = n_blocks // d
        bias = bias_ref[pi]
        bias_first = jnp.where(jnp.logical_or(kcol >= ATTN_BLK, chunk > 0), bias, neg_inf)
        for idx in range(n_blocks):
            u, res = idx // d, idx % d
            a_cur = u * run
            prev_k, prev_v, a_prev = (kc_ref, vc_ref, a_cur - run) if u > 0 else (kp_ref, vp_ref, (n_sub - 1) * run)

            def rows_of(ref, a0, lanes):
                parts = [ref[0, 0, i * d + res, a0:a0 + run, lanes] for i in range(n_runs)]
                return parts[0] if n_runs == 1 else jnp.concatenate(parts, axis=0)

            qb = rows_of(q_ref, a_cur, all_lanes)
            kb = jnp.concatenate([rows_of(prev_k, a_prev, all_lanes),
                                  rows_of(kc_ref, a_cur, all_lanes)], axis=0).astype(BF16)
            q2 = jnp.concatenate([jnp.where(head0, qb, 0.0), jnp.where(head0, 0.0, qb)], axis=0).astype(BF16)
            s2 = lax.dot_general(q2, kb, (((1,), (1,)), ((), ())), preferred_element_type=F32)
            vb = jnp.concatenate([rows_of(prev_v, a_prev, all_lanes),
                                  rows_of(vc_ref, a_cur, all_lanes)], axis=0).astype(BF16)
            parts = []
            for hsel in range(2):
                s = s2[hsel * ATTN_BLK:(hsel + 1) * ATTN_BLK] + (bias if u > 0 else bias_first)
                m = jnp.max(s, axis=1, keepdims=True)
                p = jnp.exp(s - m)
                parts.append((jnp.dot(p.astype(BF16), vb, preferred_element_type=F32),
                              m, jnp.sum(p, axis=1, keepdims=True)))
            vals = [jnp.where(head0, a, b) for a, b in zip(*parts)]
            for i in range(n_runs):
                for si, val in enumerate(vals):
                    st[si, pi, i * d + res, a_cur:a_cur + run, :] = val[i * run:(i + 1) * run]

    def merge(g, carry):
        maxes = [st[1, pi, g] for pi in range(len(DILATIONS))]
        top = functools.reduce(jnp.maximum, maxes)
        weights = [jnp.exp(mx - top) for mx in maxes]
        num = sum(w * st[0, pi, g] for pi, w in enumerate(weights))
        den = sum(w * st[2, pi, g] for pi, w in enumerate(weights))
        o_ref[0, 0, g] = (num / den).astype(o_ref.dtype)
        return carry

    lax.fori_loop(0, ATTN_GROUP, merge, 0, unroll=2)


def _top_k_gates(logits):
    expert = lax.broadcasted_iota(jnp.int32, logits.shape, 0).astype(F32)
    vals, idxs = [], []
    rest = logits
    for _ in range(TOP_K):
        best = jnp.max(rest, axis=0, keepdims=True)
        where_best = jnp.min(jnp.where(rest == best, expert, float(N_EXPERTS)), axis=0, keepdims=True)
        vals.append(best)
        idxs.append(where_best)
        rest = jnp.where(expert == where_best, -jnp.inf, rest)
    weights = [jnp.exp(v - vals[0]) for v in vals]
    total = functools.reduce(lambda a, b: a + b, weights)
    gates = jnp.concatenate([w / total for w in weights], axis=0)
    return jnp.concatenate(idxs, axis=0).astype(jnp.int32), gates


def _mix_kernel(co_ref, ao_ref, x_ref, mod_ref, wo_ref, unperm_ref, gffn_ref, wr_ref, br_ref,
                x1_ref, h2_ref, idx_ref, gate_ref):
    rows = x_ref.shape[1]
    ao = jnp.dot(unperm_ref[...], ao_ref[0, 0].reshape(rows, ATTN_WIDTH),
                 preferred_element_type=F32).astype(BF16)
    mixed = (jnp.dot(co_ref[0], wo_ref[0:CONV_CH, :], preferred_element_type=F32)
             + jnp.dot(ao, wo_ref[CONV_CH:, :], preferred_element_type=F32))
    x1 = x_ref[0] + mod_ref[0, 2:3, :] * mixed
    x1_ref[0] = x1
    h2 = _rms_modulate(x1, gffn_ref[...], mod_ref[0, 3:4, :], mod_ref[0, 4:5, :])
    for j in range(SLABS):
        h2_ref[pl.ds(j, rows, stride=SLABS), :] = h2[:, j * LANES:(j + 1) * LANES]
    logits = lax.dot_general(wr_ref[...], h2.astype(BF16), (((1,), (1,)), ((), ())),
                             preferred_element_type=F32) + br_ref[...]
    idx_ref[...], gate_ref[...] = _top_k_gates(logits)


def _slab_rows(idx):
    return pl.ds(pl.multiple_of(idx * SLABS, SLABS), SLABS)


def _expert_kernel(meta_ref, idx_hbm, h2_hbm, wgu_ref, bgu_ref, wd_ref, bd_ref, y_hbm,
                   xbuf, ybuf, idx_a, idx_b, wgu_bf, wd_bf, gsem, ssem, isem):
    i = pl.program_id(0)
    n_used = meta_ref[0]
    expert = meta_ref[1 + i]
    prev_expert = meta_ref[jnp.maximum(i, 1)]
    cur = i % 2
    idx_bufs = (idx_a, idx_b)
    idx_row = 2 * MOE_ROWS

    def idx_copy(row, slot):
        start = pl.multiple_of(row * idx_row, idx_row)
        return pltpu.make_async_copy(idx_hbm.at[pl.ds(start, idx_row)], idx_bufs[slot], isem)

    def row_slab(r):
        return pl.ds(r * SLABS, SLABS) if isinstance(r, int) else _slab_rows(r)

    def gather_copy(idx_slot, buf_slot, r):
        return pltpu.make_async_copy(h2_hbm.at[_slab_rows(idx_bufs[idx_slot][r]), :],
                                     xbuf.at[buf_slot, row_slab(r), :], gsem)

    def scatter_copy(idx_slot, buf_slot, r):
        return pltpu.make_async_copy(ybuf.at[buf_slot, row_slab(r), :],
                                     y_hbm.at[_slab_rows(idx_bufs[idx_slot][MOE_ROWS + r]), :], ssem)

    def start_rows(idx_slot, buf_slot, gather, scatter, unrolled):
        def body(r, carry):
            if gather:
                gather_copy(idx_slot, buf_slot, r).start()
            if scatter:
                scatter_copy(idx_slot, buf_slot, r).start()
            return carry
        if unrolled:
            for r in range(MOE_ROWS):
                body(r, 0)
        else:
            lax.fori_loop(0, MOE_ROWS, body, 0, unroll=8)

    def wait_gathers():
        pltpu.make_async_copy(h2_hbm.at[pl.ds(0, MOE_ROWS * SLABS), :], xbuf.at[0], gsem).wait()

    def wait_scatters():
        pltpu.make_async_copy(ybuf.at[0], y_hbm.at[pl.ds(0, MOE_ROWS * SLABS), :], ssem).wait()

    @pl.when(i == 0)
    def _():
        first = idx_copy(0, 1)
        first.start()
        first.wait()
        start_rows(1, 0, True, False, False)
        wait_gathers()
        ybuf[1] = jnp.zeros((MOE_ROWS * SLABS, LANES), F32)
        second = idx_copy(1, 0)
        second.start()
        second.wait()

    @pl.when(jnp.logical_or(i == 0, expert != prev_expert))
    def _():
        wgu_bf[...] = wgu_ref[0].astype(BF16)
        wd_bf[...] = wd_ref[0].astype(BF16)

    for parity in (0, 1):
        @pl.when(jnp.logical_and(i < n_used, cur == parity))
        def _(parity=parity):
            idx_copy(i + 2, 1 - parity).start()
            start_rows(parity, 1 - parity, True, True, True)

        @pl.when(jnp.logical_and(i == n_used, cur == parity))
        def _(parity=parity):
            start_rows(parity, 1 - parity, False, True, False)
            wait_scatters()

    @pl.when(i < n_used)
    def _():
        x = jnp.concatenate([xbuf[cur, pl.ds(j, MOE_ROWS, stride=SLABS), :] for j in range(SLABS)],
                            axis=1).astype(BF16)
        y = jnp.broadcast_to(bd_ref[0], (MOE_ROWS, D_MODEL))
        for cidx in range(D_FF // MOE_COLS):
            lo = cidx * MOE_COLS
            gate = jnp.dot(x, wgu_bf[:, lo:lo + MOE_COLS], preferred_element_type=F32) + bgu_ref[0, :, lo:lo + MOE_COLS]
            up = (jnp.dot(x, wgu_bf[:, D_FF + lo:D_FF + lo + MOE_COLS], preferred_element_type=F32)
                  + bgu_ref[0, :, D_FF + lo:D_FF + lo + MOE_COLS])
            gate = jnp.minimum(gate, SWIGLU_LIMIT)
            up = jnp.clip(up, -SWIGLU_LIMIT, SWIGLU_LIMIT)
            act = (up + 1.0) * (gate * jax.nn.sigmoid(SWIGLU_ALPHA * gate))
            y = y + jnp.dot(act.astype(BF16), wd_bf[lo:lo + MOE_COLS, :], preferred_element_type=F32)
        for j in range(SLABS):
            ybuf[cur, pl.ds(j, MOE_ROWS, stride=SLABS), :] = y[:, j * LANES:(j + 1) * LANES]
        wait_gathers()
        wait_scatters()
        idx_copy(0, 0).wait()


def _combine_kernel(gate_ref, x1_ref, mod_ref, gfin_ref, y_ref, o_ref, slab):
    base = pl.program_id(0) * (TOK_TILE * TOP_K)

    def token(t, carry):
        acc = None
        for k in range(TOP_K):
            a = t * TOP_K + k
            term = gate_ref[base + a] * y_ref[_slab_rows(a), :]
            acc = term if acc is None else acc + term
        slab[_slab_rows(t), :] = acc
        return carry

    lax.fori_loop(0, TOK_TILE, token, 0, unroll=8)
    ffn = jnp.concatenate([slab[pl.ds(j, TOK_TILE, stride=SLABS), :] for j in range(SLABS)], axis=1)
    x2 = x1_ref[...] + mod_ref[0, 5:6, :] * ffn
    ms = jnp.mean(x2 * x2, axis=-1, keepdims=True)
    o_ref[...] = x2 * lax.rsqrt(ms + NORM_EPS) * gfin_ref[...]


def _rotary_cos_sin(positions):
    half = ROT_DIM // 2
    inv_freq = ROPE_THETA ** (-jnp.arange(0, ROT_DIM, 2, dtype=F32) / ROT_DIM)
    ang = inv_freq.reshape((half,) + (1,) * positions.ndim) * positions.astype(F32)[None]
    return jnp.moveaxis(jnp.concatenate([jnp.cos(ang), jnp.sin(ang)], axis=0), 0, -1)


def _rotary_expansion():
    half = ROT_DIM // 2
    expand = np.zeros((ROT_DIM, 3 * LANES), np.float32)
    offset = np.zeros((1, 3 * LANES), np.float32)
    for lane in range(LANES):
        dim = lane % HEAD_DIM
        if dim < ROT_DIM:
            expand[dim % half, lane] = 1.0
            if dim < half:
                expand[half + dim, LANES + lane] = -1.0
            else:
                expand[half + dim - half, 2 * LANES + lane] = 1.0
        else:
            offset[0, lane] = 1.0
    return jnp.asarray(expand), jnp.asarray(offset)


def _group_perm(rows):
    run = rows // ATTN_GROUP
    dst = np.arange(rows)
    perm = np.zeros((rows, rows), np.float32)
    perm[dst, ATTN_GROUP * (dst % run) + dst // run] = 1.0
    return jnp.asarray(perm, BF16)


def _band_bias():
    pos = np.arange(ATTN_BLK)
    tables = []
    for d in DILATIONS:
        n_runs = ATTN_GROUP // d
        run = ATTN_BLK // n_runs
        strided = (pos % run) * n_runs + pos // run
        dist = strided[:, None] + ATTN_BLK - np.concatenate([strided, strided + ATTN_BLK])[None, :]
        tables.append(np.where((dist >= 0) & (dist <= ATTN_BLK), 0.0, -np.inf))
    return jnp.asarray(np.stack(tables), F32)


def _attention(q, k, v):
    bsz, n_chunks, _, grp_rows, _ = q.shape

    def chunk_spec(width, back):
        return pl.BlockSpec((1, 1, ATTN_GROUP, grp_rows, width),
                            lambda b, p, ch: (b, jnp.maximum(ch - back, 0), 0, 0, p))
    return pl.pallas_call(
        _attn_kernel,
        grid=(bsz, ATTN_WIDTH // LANES, n_chunks),
        in_specs=[pl.BlockSpec((len(DILATIONS), ATTN_BLK, 2 * ATTN_BLK), lambda b, p, ch: (0, 0, 0)),
                  chunk_spec(LANES, 0), chunk_spec(LANES, 1), chunk_spec(LANES, 0),
                  chunk_spec(LANES, 1), chunk_spec(LANES, 0)],
        out_specs=chunk_spec(LANES, 0),
        out_shape=jax.ShapeDtypeStruct(q.shape, BF16),
        scratch_shapes=[pltpu.VMEM((3, len(DILATIONS), ATTN_GROUP, grp_rows, LANES), F32)],
        compiler_params=pltpu.CompilerParams(dimension_semantics=("arbitrary",) * 3,
                                             vmem_limit_bytes=VMEM_LIMIT),
        name="attn",
    )(_band_bias(), q, k, k, v, v)


def _experts(meta, idx, h2, w_gu, b_gu, w_down, b_down, n_blocks, n_slots):
    d = w_down.shape[-1]

    def expert_block(i, meta):
        return (meta[1 + i], 0, 0)

    any_spec = pl.BlockSpec(memory_space=pl.ANY)
    return pl.pallas_call(
        _expert_kernel,
        grid_spec=pltpu.PrefetchScalarGridSpec(
            num_scalar_prefetch=1,
            grid=(n_blocks + 1,),
            in_specs=[any_spec, any_spec,
                      pl.BlockSpec((1, d, 2 * D_FF), expert_block),
                      pl.BlockSpec((1, 1, 2 * D_FF), expert_block),
                      pl.BlockSpec((1, D_FF, d), expert_block),
                      pl.BlockSpec((1, 1, d), expert_block)],
            out_specs=any_spec,
            scratch_shapes=[pltpu.VMEM((2, MOE_ROWS * SLABS, LANES), F32),
                            pltpu.VMEM((2, MOE_ROWS * SLABS, LANES), F32),
                            pltpu.SMEM((2 * MOE_ROWS,), jnp.int32), pltpu.SMEM((2 * MOE_ROWS,), jnp.int32),
                            pltpu.VMEM((d, 2 * D_FF), BF16), pltpu.VMEM((D_FF, d), BF16),
                            pltpu.SemaphoreType.DMA(()), pltpu.SemaphoreType.DMA(()),
                            pltpu.SemaphoreType.DMA(())]),
        out_shape=jax.ShapeDtypeStruct((n_slots * SLABS, LANES), F32),
        compiler_params=pltpu.CompilerParams(dimension_semantics=("arbitrary",), vmem_limit_bytes=VMEM_LIMIT),
        name="experts",
    )(meta, idx, h2, w_gu, b_gu.reshape(N_EXPERTS, 1, 2 * D_FF), w_down, b_down.reshape(N_EXPERTS, 1, d))


def _routing(top_idx):
    n_tok = top_idx.shape[1]
    n_asg = n_tok * TOP_K
    i32 = jnp.int32
    slot_of = jnp.arange(n_tok, dtype=i32)[None, :] * TOP_K + jnp.arange(TOP_K, dtype=i32)[:, None]
    key = (top_idx.astype(i32) * n_asg + slot_of).reshape(n_asg)
    skey = lax.sort(key)
    asg_sorted = skey % n_asg
    experts = jnp.arange(N_EXPERTS, dtype=i32)
    counts = jnp.sum((top_idx.reshape(1, n_asg) == experts[:, None]).astype(i32), axis=1)
    starts = jnp.cumsum(counts) - counts
    padded = (counts + MOE_ROWS - 1) // MOE_ROWS * MOE_ROWS
    pends = jnp.cumsum(padded)
    pstarts = pends - padded
    n_blocks = n_asg // MOE_ROWS + N_EXPERTS
    n_used = (pends[-1] // MOE_ROWS).astype(i32)
    blk = jnp.arange(n_blocks + 2, dtype=i32)
    blk_start = blk * MOE_ROWS
    clamped = jnp.minimum(blk_start, pends[-1] - MOE_ROWS)
    blk_expert = jnp.sum((pends[None, :] <= clamped[:, None]).astype(i32), axis=1)
    row = jnp.arange(MOE_ROWS, dtype=i32)[None, :]
    off = blk_start[:, None] + row - pstarts[blk_expert][:, None]
    valid = jnp.logical_and(off < counts[blk_expert][:, None], blk[:, None] < n_used)
    asg = asg_sorted[jnp.clip(starts[blk_expert][:, None] + off, 0, n_asg - 1)]
    tok = jnp.where(valid, asg // TOP_K, 0)
    spare = jnp.broadcast_to(n_asg + row, (2, MOE_ROWS))
    slot = jnp.concatenate([spare, jnp.where(valid, asg, n_asg + row)[:-2]], axis=0)
    idx = jnp.concatenate([tok, slot], axis=1).reshape(-1).astype(i32)
    meta = jnp.concatenate([n_used[None], blk_expert[:n_blocks + 1]])
    return idx, meta, n_blocks


def kernel(x, c, positions, w_ada, b_ada, g_mix, w_in, conv_w, conv_b, conv_ln_g, conv_ln_b,
           w_out, g_ffn, w_router, b_router, w_gu, b_gu, w_down, b_down, g_final):
    bsz, seq, d = x.shape
    n_tok = bsz * seq
    depth = w_ada.shape[0]
    assert depth == 1, "the final RMSNorm is fused into the layer's combine kernel"
    n_chunks = seq // ATTN_CHUNK
    grp_rows = ATTN_CHUNK // ATTN_GROUP
    tile_run = PROJ_ROWS // ATTN_GROUP
    tiles_per_chunk = ATTN_CHUNK // PROJ_ROWS
    grouped_pos = positions.reshape(bsz, n_chunks, grp_rows, ATTN_GROUP).swapaxes(2, 3)
    cos_sin = _rotary_cos_sin(grouped_pos)
    rot_expand, rot_offset = _rotary_expansion()
    perm = _group_perm(PROJ_ROWS)
    s_tiles = seq // PROJ_ROWS
    arb2 = pltpu.CompilerParams(dimension_semantics=("arbitrary", "arbitrary"), vmem_limit_bytes=VMEM_LIMIT)
    arb1 = pltpu.CompilerParams(dimension_semantics=("arbitrary",), vmem_limit_bytes=VMEM_LIMIT)

    for l in range(depth):
        n_mod = 6 * d
        mod = pl.pallas_call(
            _ada_kernel,
            grid=(n_mod // d,),
            in_specs=[pl.BlockSpec((bsz, d), lambda j: (0, 0)),
                      pl.BlockSpec((d, d), lambda j: (0, j)),
                      pl.BlockSpec((1, d), lambda j: (0, j))],
            out_specs=pl.BlockSpec((bsz, d), lambda j: (0, j)),
            out_shape=jax.ShapeDtypeStruct((bsz, n_mod), F32),
            name="ada",
        )(c, w_ada[l], b_ada[l].reshape(1, n_mod))
        mod = mod.reshape(bsz, 6, d)

        row_spec = lambda width: pl.BlockSpec((1, PROJ_ROWS, width), lambda b, s: (b, s, 0))
        grp_spec = lambda width: pl.BlockSpec(
            (1, 1, ATTN_GROUP, tile_run, width), lambda b, s: (b, s // tiles_per_chunk, 0, s % tiles_per_chunk, 0))
        grp_shape = lambda width, dtype: jax.ShapeDtypeStruct((bsz, n_chunks, ATTN_GROUP, grp_rows, width), dtype)
        full2 = lambda shape: pl.BlockSpec(shape, lambda b, s: (0, 0))
        mod_spec = pl.BlockSpec((1, 6, d), lambda b, s: (b, 0, 0))
        conv_out, q, k, v = pl.pallas_call(
            _proj_kernel,
            grid=(bsz, s_tiles),
            in_specs=[row_spec(d), mod_spec, full2((1, d)), full2((d, IN_COLS)), full2((PROJ_ROWS, PROJ_ROWS)),
                      full2((CONV_WIDTH, CONV_CH)), full2((1, CONV_CH)), full2((1, CONV_CH)), full2((1, CONV_CH)),
                      grp_spec(ROT_DIM), full2((ROT_DIM, 3 * LANES)), full2((1, 3 * LANES))],
            out_specs=[row_spec(CONV_CH), grp_spec(ATTN_WIDTH), grp_spec(ATTN_WIDTH), grp_spec(ATTN_WIDTH)],
            out_shape=[jax.ShapeDtypeStruct((bsz, seq, CONV_CH), BF16),
                       grp_shape(ATTN_WIDTH, F32), grp_shape(ATTN_WIDTH, F32), grp_shape(ATTN_WIDTH, F32)],
            scratch_shapes=[pltpu.VMEM((PROJ_ROWS + CONV_HALO, CONV_CH), F32)],
            compiler_params=arb2,
            name="proj",
        )(x, mod, g_mix[l].reshape(1, d), w_in[l].astype(BF16), perm, conv_w[l].reshape(CONV_WIDTH, CONV_CH),
          conv_b[l].reshape(1, CONV_CH), conv_ln_g[l].reshape(1, CONV_CH), conv_ln_b[l].reshape(1, CONV_CH),
          cos_sin, rot_expand, rot_offset)

        attn_out = _attention(q, k, v)

        tok_cols = pl.BlockSpec((TOP_K, PROJ_ROWS), lambda b, s: (0, b * s_tiles + s))
        x1, h2, top_idx, gates = pl.pallas_call(
            _mix_kernel,
            grid=(bsz, s_tiles),
            in_specs=[row_spec(CONV_CH), grp_spec(ATTN_WIDTH), row_spec(d), mod_spec, full2((d, d)),
                      full2((PROJ_ROWS, PROJ_ROWS)), full2((1, d)), full2((N_EXPERTS, d)), full2((N_EXPERTS, 1))],
            out_specs=[row_spec(d),
                       pl.BlockSpec((PROJ_ROWS * SLABS, LANES), lambda b, s: (b * s_tiles + s, 0)),
                       tok_cols, tok_cols],
            out_shape=[jax.ShapeDtypeStruct((bsz, seq, d), F32),
                       jax.ShapeDtypeStruct((n_tok * SLABS, LANES), F32),
                       jax.ShapeDtypeStruct((TOP_K, n_tok), jnp.int32),
                       jax.ShapeDtypeStruct((TOP_K, n_tok), F32)],
            compiler_params=arb2,
            name="mix",
        )(conv_out, attn_out, x, mod, w_out[l].astype(BF16), perm.T, g_ffn[l].reshape(1, d),
          w_router[l].T.astype(BF16), b_router[l].reshape(N_EXPERTS, 1))
        gates = gates.T

        idx, meta, n_blocks = _routing(top_idx)
        n_slots = n_tok * TOP_K + MOE_ROWS
        n_tiles = n_tok // TOK_TILE

        y = _experts(meta, idx, h2, w_gu[l], b_gu[l], w_down[l], b_down[l], n_blocks, n_slots)

        out = pl.pallas_call(
            _combine_kernel,
            grid_spec=pltpu.PrefetchScalarGridSpec(
                num_scalar_prefetch=1,
                grid=(n_tiles,),
                in_specs=[pl.BlockSpec((TOK_TILE, d), lambda i, g: (i, 0)),
                          pl.BlockSpec((1, 6, d), lambda i, g: (i // (seq // TOK_TILE), 0, 0)),
                          pl.BlockSpec((1, d), lambda i, g: (0, 0)),
                          pl.BlockSpec((TOK_TILE * TOP_K * SLABS, LANES), lambda i, g: (i, 0))],
                out_specs=pl.BlockSpec((TOK_TILE, d), lambda i, g: (i, 0)),
                scratch_shapes=[pltpu.VMEM((TOK_TILE * SLABS, LANES), F32)]),
            out_shape=jax.ShapeDtypeStruct((n_tok, d), F32),
            compiler_params=arb1,
            name="combine",
        )(gates.reshape(-1), x1.reshape(n_tok, d), mod, g_final.reshape(1, d), y)
        x = out.reshape(bsz, seq, d)
    return x
```

```python
import functools

import jax
import numpy as np
import jax.numpy as jnp
from jax import lax
from jax.experimental import pallas as pl
from jax.experimental.pallas import tpu as pltpu

F32 = jnp.float32
BF16 = jnp.bfloat16

D_MODEL = 1024
N_HEADS = 8
HEAD_DIM = 64
ATTN_WIDTH = N_HEADS * HEAD_DIM
CONV_CH = D_MODEL - ATTN_WIDTH
IN_COLS = 2 * CONV_CH + 3 * ATTN_WIDTH
CONV_WIDTH = 31
ROT_DIM = HEAD_DIM // 4
ROPE_THETA = 500000.0
DILATIONS = (1, 4, 16)
ATTN_BLK = 128
ATTN_GROUP = 16
N_EXPERTS = 32
TOP_K = 4
D_FF = D_MODEL
SWIGLU_LIMIT = 7.0
SWIGLU_ALPHA = 1.702
NORM_EPS = 1e-6

LANES = 128
SUBLANES = 8
SLABS = D_MODEL // LANES
VMEM_LIMIT = 56 * 1024 * 1024

PROJ_ROWS = 512
CONV_HALO = 32
CONV_ROWS = 64
ATTN_CHUNK = 2048
MOE_ROWS = 512
MOE_COLS = 512
TOK_TILE = 256


def _ada_kernel(c_ref, w_ref, b_ref, o_ref):
    c = c_ref[...]
    c_act = c * jax.nn.sigmoid(c)
    o_ref[...] = jnp.dot(c_act.astype(BF16), w_ref[...].astype(BF16), preferred_element_type=F32) + b_ref[...]


def _rms_modulate(x, gain, shift, scale):
    ms = jnp.mean(x * x, axis=-1, keepdims=True)
    y = x * lax.rsqrt(ms + NORM_EPS) * gain
    return y * (1.0 + scale) + shift


def _proj_kernel(x_ref, mod_ref, gmix_ref, win_ref, perm_ref, cw_ref, cb_ref, lng_ref, lnb_ref,
                 cs_ref, rexp_ref, roff_ref, co_ref, q_ref, k_ref, v_ref, ubuf):
    rows = x_ref.shape[1]
    hb = _rms_modulate(x_ref[0], gmix_ref[...], mod_ref[0, 0:1, :], mod_ref[0, 1:2, :]).astype(BF16)

    ag = jnp.dot(hb, win_ref[:, :2 * CONV_CH], preferred_element_type=F32)
    u = ag[:, :CONV_CH] * jax.nn.sigmoid(ag[:, CONV_CH:])

    @pl.when(pl.program_id(1) == 0)
    def _():
        ubuf[0:CONV_HALO, :] = jnp.zeros((CONV_HALO, CONV_CH), F32)

    ubuf[CONV_HALO:, :] = u
    first_tap = CONV_HALO - (CONV_WIDTH - 1)
    for ch in range(rows // CONV_ROWS):
        row0 = ch * CONV_ROWS
        acc = jnp.broadcast_to(cb_ref[...], (CONV_ROWS, CONV_CH))
        for shift in range(SUBLANES):
            taps = [j for j in range(CONV_WIDTH) if (first_tap + j) % SUBLANES == shift]
            span = CONV_ROWS + (SUBLANES if shift else 0)
            part = None
            for j in taps:
                base = row0 + first_tap + j - shift
                term = cw_ref[j:j + 1, :] * ubuf[base:base + span, :]
                part = term if part is None else part + term
            acc = acc + part[shift:shift + CONV_ROWS]
        mu = jnp.mean(acc, axis=-1, keepdims=True)
        cen = acc - mu
        var = jnp.mean(cen * cen, axis=-1, keepdims=True)
        yn = cen * lax.rsqrt(var + NORM_EPS) * lng_ref[...] + lnb_ref[...]
        co_ref[0, ch * CONV_ROWS:(ch + 1) * CONV_ROWS, :] = (yn * jax.nn.sigmoid(yn)).astype(co_ref.dtype)
    ubuf[0:CONV_HALO, :] = ubuf[rows:rows + CONV_HALO, :]

    hp = jnp.dot(perm_ref[...], hb, preferred_element_type=F32).astype(BF16)
    qkv = jnp.dot(hp, win_ref[:, 2 * CONV_CH:], preferred_element_type=F32)
    run = rows // ATTN_GROUP
    tile = (ATTN_GROUP, run, LANES)
    cs = cs_ref[0, 0].reshape(rows, ROT_DIM)
    hi = cs.astype(BF16)
    rest = cs - hi.astype(F32)
    mid = rest.astype(BF16)
    low = (rest - mid.astype(F32)).astype(BF16)
    tables = jnp.dot(jnp.concatenate([hi, mid, low], axis=1), rexp_ref[...],
                     preferred_element_type=F32) + roff_ref[...]
    rc, rsa, rsb = (tables[:, i * LANES:(i + 1) * LANES] for i in range(3))
    half = ROT_DIM // 2
    for blk in range(ATTN_WIDTH // LANES):
        lo, hi = blk * LANES, (blk + 1) * LANES
        for src0, dst_ref, scale in ((0, q_ref, HEAD_DIM ** -0.5), (ATTN_WIDTH, k_ref, 1.0)):
            t = qkv[:, src0 + lo:src0 + hi]
            rot = t * rc + pltpu.roll(t, LANES - half, 1) * rsa + pltpu.roll(t, half, 1) * rsb
            dst_ref[0, 0, :, :, lo:hi] = (rot * scale).reshape(tile)
        v_ref[0, 0, :, :, lo:hi] = qkv[:, 2 * ATTN_WIDTH + lo:2 * ATTN_WIDTH + hi].reshape(tile)


def _attn_kernel(bias_ref, q_ref, kp_ref, kc_ref, vp_ref, vc_ref, o_ref, st):
    chunk = pl.program_id(2)
    lane = lax.broadcasted_iota(jnp.int32, (ATTN_BLK, LANES), 1)
    head0 = lane < HEAD_DIM
    kcol = lax.broadcasted_iota(jnp.int32, (ATTN_BLK, 2 * ATTN_BLK), 1)
    neg_inf = jnp.float32(-jnp.inf)
    n_blocks = ATTN_CHUNK // ATTN_BLK
    all_lanes = slice(None)

    for pi, d in enumerate(DILATIONS):
        n_runs = ATTN_GROUP // d
        run = ATTN_BLK // n_runs
        n_sub = n_blocks // d
        bias = bias_ref[pi]
        bias_first = jnp.where(jnp.logical_or(kcol >= ATTN_BLK, chunk > 0), bias, neg_inf)
        for idx in range(n_blocks):
            u, res = idx // d, idx % d
            a_cur = u * run
            prev_k, prev_v, a_prev = (kc_ref, vc_ref, a_cur - run) if u > 0 else (kp_ref, vp_ref, (n_sub - 1) * run)

            def rows_of(ref, a0, lanes):
                parts = [ref[0, 0, i * d + res, a0:a0 + run, lanes] for i in range(n_runs)]
                return parts[0] if n_runs == 1 else jnp.concatenate(parts, axis=0)

            qb = rows_of(q_ref, a_cur, all_lanes)
            kb = jnp.concatenate([rows_of(prev_k, a_prev, all_lanes),
                                  rows_of(kc_ref, a_cur, all_lanes)], axis=0).astype(BF16)
            q2 = jnp.concatenate([jnp.where(head0, qb, 0.0), jnp.where(head0, 0.0, qb)], axis=0).astype(BF16)
            s2 = lax.dot_general(q2, kb, (((1,), (1,)), ((), ())), preferred_element_type=F32)
            vb = jnp.concatenate([rows_of(prev_v, a_prev, all_lanes),
                                  rows_of(vc_ref, a_cur, all_lanes)], axis=0).astype(BF16)
            parts = []
            for hsel in range(2):
                s = s2[hsel * ATTN_BLK:(hsel + 1) * ATTN_BLK] + (bias if u > 0 else bias_first)
                m = jnp.max(s, axis=1, keepdims=True)
                p = jnp.exp(s - m)
                parts.append((jnp.dot(p.astype(BF16), vb, preferred_element_type=F32),
                              m, jnp.sum(p, axis=1, keepdims=True)))
            vals = [jnp.where(head0, a, b) for a, b in zip(*parts)]
            for i in range(n_runs):
                for si, val in enumerate(vals):
                    st[si, pi, i * d + res, a_cur:a_cur + run, :] = val[i * run:(i + 1) * run]

    def merge(g, carry):
        maxes = [st[1, pi, g] for pi in range(len(DILATIONS))]
        top = functools.reduce(jnp.maximum, maxes)
        weights = [jnp.exp(mx - top) for mx in maxes]
        num = sum(w * st[0, pi, g] for pi, w in enumerate(weights))
        den = sum(w * st[2, pi, g] for pi, w in enumerate(weights))
        o_ref[0, 0, g] = (num / den).astype(o_ref.dtype)
        return carry

    lax.fori_loop(0, ATTN_GROUP, merge, 0, unroll=2)


def _top_k_gates(logits):
    expert = lax.broadcasted_iota(jnp.int32, logits.shape, 0).astype(F32)
    vals, idxs = [], []
    rest = logits
    for _ in range(TOP_K):
        best = jnp.max(rest, axis=0, keepdims=True)
        where_best = jnp.min(jnp.where(rest == best, expert, float(N_EXPERTS)), axis=0, keepdims=True)
        vals.append(best)
        idxs.append(where_best)
        rest = jnp.where(expert == where_best, -jnp.inf, rest)
    weights = [jnp.exp(v - vals[0]) for v in vals]
    total = functools.reduce(lambda a, b: a + b, weights)
    gates = jnp.concatenate([w / total for w in weights], axis=0)
    return jnp.concatenate(idxs, axis=0).astype(jnp.int32), gates


def _mix_kernel(co_ref, ao_ref, x_ref, mod_ref, wo_ref, unperm_ref, gffn_ref, wr_ref, br_ref,
                x1_ref, h2_ref, idx_ref, gate_ref):
    rows = x_ref.shape[1]
    ao = jnp.dot(unperm_ref[...], ao_ref[0, 0].reshape(rows, ATTN_WIDTH),
                 preferred_element_type=F32).astype(BF16)
    mixed = (jnp.dot(co_ref[0], wo_ref[0:CONV_CH, :], preferred_element_type=F32)
             + jnp.dot(ao, wo_ref[CONV_CH:, :], preferred_element_type=F32))
    x1 = x_ref[0] + mod_ref[0, 2:3, :] * mixed
    x1_ref[0] = x1
    h2 = _rms_modulate(x1, gffn_ref[...], mod_ref[0, 3:4, :], mod_ref[0, 4:5, :])
    for j in range(SLABS):
        h2_ref[pl.ds(j, rows, stride=SLABS), :] = h2[:, j * LANES:(j + 1) * LANES]
    logits = lax.dot_general(wr_ref[...], h2.astype(BF16), (((1,), (1,)), ((), ())),
                             preferred_element_type=F32) + br_ref[...]
    idx_ref[...], gate_ref[...] = _top_k_gates(logits)


def _slab_rows(idx):
    return pl.ds(pl.multiple_of(idx * SLABS, SLABS), SLABS)


def _expert_kernel(meta_ref, idx_hbm, h2_hbm, wgu_ref, bgu_ref, wd_ref, bd_ref, y_hbm,
                   xbuf, ybuf, idx_a, idx_b, wgu_bf, wd_bf, gsem, ssem, isem):
    i = pl.program_id(0)
    n_used = meta_ref[0]
    expert = meta_ref[1 + i]
    prev_expert = meta_ref[jnp.maximum(i, 1)]
    cur = i % 2
    idx_bufs = (idx_a, idx_b)
    idx_row = 2 * MOE_ROWS

    def idx_copy(row, slot):
        start = pl.multiple_of(row * idx_row, idx_row)
        return pltpu.make_async_copy(idx_hbm.at[pl.ds(start, idx_row)], idx_bufs[slot], isem)

    def row_slab(r):
        return pl.ds(r * SLABS, SLABS) if isinstance(r, int) else _slab_rows(r)

    def gather_copy(idx_slot, buf_slot, r):
        return pltpu.make_async_copy(h2_hbm.at[_slab_rows(idx_bufs[idx_slot][r]), :],
                                     xbuf.at[buf_slot, row_slab(r), :], gsem)

    def scatter_copy(idx_slot, buf_slot, r):
        return pltpu.make_async_copy(ybuf.at[buf_slot, row_slab(r), :],
                                     y_hbm.at[_slab_rows(idx_bufs[idx_slot][MOE_ROWS + r]), :], ssem)

    def start_rows(idx_slot, buf_slot, gather, scatter, unrolled):
        def body(r, carry):
            if gather:
                gather_copy(idx_slot, buf_slot, r).start()
            if scatter:
                scatter_copy(idx_slot, buf_slot, r).start()
            return carry
        if unrolled:
            for r in range(MOE_ROWS):
                body(r, 0)
        else:
            lax.fori_loop(0, MOE_ROWS, body, 0, unroll=8)

    def wait_gathers():
        pltpu.make_async_copy(h2_hbm.at[pl.ds(0, MOE_ROWS * SLABS), :], xbuf.at[0], gsem).wait()

    def wait_scatters():
        pltpu.make_async_copy(ybuf.at[0], y_hbm.at[pl.ds(0, MOE_ROWS * SLABS), :], ssem).wait()

    @pl.when(i == 0)
    def _():
        first = idx_copy(0, 1)
        first.start()
        first.wait()
        start_rows(1, 0, True, False, False)
        wait_gathers()
        ybuf[1] = jnp.zeros((MOE_ROWS * SLABS, LANES), F32)
        second = idx_copy(1, 0)
        second.start()
        second.wait()

    @pl.when(jnp.logical_or(i == 0, expert != prev_expert))
    def _():
        wgu_bf[...] = wgu_ref[0].astype(BF16)
        wd_bf[...] = wd_ref[0].astype(BF16)

    for parity in (0, 1):
        @pl.when(jnp.logical_and(i < n_used, cur == parity))
        def _(parity=parity):
            idx_copy(i + 2, 1 - parity).start()
            start_rows(parity, 1 - parity, True, True, True)

        @pl.when(jnp.logical_and(i == n_used, cur == parity))
        def _(parity=parity):
            start_rows(parity, 1 - parity, False, True, False)
            wait_scatters()

    @pl.when(i < n_used)
    def _():
        x = jnp.concatenate([xbuf[cur, pl.ds(j, MOE_ROWS, stride=SLABS), :] for j in range(SLABS)],
                            axis=1).astype(BF16)
        y = jnp.broadcast_to(bd_ref[0], (MOE_ROWS, D_MODEL))
        for cidx in range(D_FF // MOE_COLS):
            lo = cidx * MOE_COLS
            gate = jnp.dot(x, wgu_bf[:, lo:lo + MOE_COLS], preferred_element_type=F32) + bgu_ref[0, :, lo:lo + MOE_COLS]
            up = (jnp.dot(x, wgu_bf[:, D_FF + lo:D_FF + lo + MOE_COLS], preferred_element_type=F32)
                  + bgu_ref[0, :, D_FF + lo:D_FF + lo + MOE_COLS])
            gate = jnp.minimum(gate, SWIGLU_LIMIT)
            up = jnp.clip(up, -SWIGLU_LIMIT, SWIGLU_LIMIT)
            act = (up + 1.0) * (gate * jax.nn.sigmoid(SWIGLU_ALPHA * gate))
            y = y + jnp.dot(act.astype(BF16), wd_bf[lo:lo + MOE_COLS, :], preferred_element_type=F32)
        for j in range(SLABS):
            ybuf[cur, pl.ds(j, MOE_ROWS, stride=SLABS), :] = y[:, j * LANES:(j + 1) * LANES]
        wait_gathers()
        wait_scatters()
        idx_copy(0, 0).wait()


def _combine_kernel(gate_ref, x1_ref, mod_ref, gfin_ref, y_ref, o_ref, slab):
    base = pl.program_id(0) * (TOK_TILE * TOP_K)

    def token(t, carry):
        acc = None
        for k in range(TOP_K):
            a = t * TOP_K + k
            term = gate_ref[base + a] * y_ref[_slab_rows(a), :]
            acc = term if acc is None else acc + term
        slab[_slab_rows(t), :] = acc
        return carry

    lax.fori_loop(0, TOK_TILE, token, 0, unroll=8)
    ffn = jnp.concatenate([slab[pl.ds(j, TOK_TILE, stride=SLABS), :] for j in range(SLABS)], axis=1)
    x2 = x1_ref[...] + mod_ref[0, 5:6, :] * ffn
    ms = jnp.mean(x2 * x2, axis=-1, keepdims=True)
    o_ref[...] = x2 * lax.rsqrt(ms + NORM_EPS) * gfin_ref[...]


def _rotary_cos_sin(positions):
    half = ROT_DIM // 2
    inv_freq = ROPE_THETA ** (-jnp.arange(0, ROT_DIM, 2, dtype=F32) / ROT_DIM)
    ang = inv_freq.reshape((half,) + (1,) * positions.ndim) * positions.astype(F32)[None]
    return jnp.moveaxis(jnp.concatenate([jnp.cos(ang), jnp.sin(ang)], axis=0), 0, -1)


def _rotary_expansion():
    half = ROT_DIM // 2
    expand = np.zeros((ROT_DIM, 3 * LANES), np.float32)
    offset = np.zeros((1, 3 * LANES), np.float32)
    for lane in range(LANES):
        dim = lane % HEAD_DIM
        if dim < ROT_DIM:
            expand[dim % half, lane] = 1.0
            if dim < half:
                expand[half + dim, LANES + lane] = -1.0
            else:
                expand[half + dim - half, 2 * LANES + lane] = 1.0
        else:
            offset[0, lane] = 1.0
    return jnp.asarray(np.tile(expand, (3, 1)), BF16), jnp.asarray(offset)


def _group_perm(rows):
    run = rows // ATTN_GROUP
    dst = np.arange(rows)
    perm = np.zeros((rows, rows), np.float32)
    perm[dst, ATTN_GROUP * (dst % run) + dst // run] = 1.0
    return jnp.asarray(perm, BF16)


def _band_bias():
    pos = np.arange(ATTN_BLK)
    tables = []
    for d in DILATIONS:
        n_runs = ATTN_GROUP // d
        run = ATTN_BLK // n_runs
        strided = (pos % run) * n_runs + pos // run
        dist = strided[:, None] + ATTN_BLK - np.concatenate([strided, strided + ATTN_BLK])[None, :]
        tables.append(np.where((dist >= 0) & (dist <= ATTN_BLK), 0.0, -np.inf))
    return jnp.asarray(np.stack(tables), F32)


def _attention(q, k, v):
    bsz, n_chunks, _, grp_rows, _ = q.shape

    def chunk_spec(width, back):
        return pl.BlockSpec((1, 1, ATTN_GROUP, grp_rows, width),
                            lambda b, p, ch: (b, jnp.maximum(ch - back, 0), 0, 0, p))
    return pl.pallas_call(
        _attn_kernel,
        grid=(bsz, ATTN_WIDTH // LANES, n_chunks),
        in_specs=[pl.BlockSpec((len(DILATIONS), ATTN_BLK, 2 * ATTN_BLK), lambda b, p, ch: (0, 0, 0)),
                  chunk_spec(LANES, 0), chunk_spec(LANES, 1), chunk_spec(LANES, 0),
                  chunk_spec(LANES, 1), chunk_spec(LANES, 0)],
        out_specs=chunk_spec(LANES, 0),
        out_shape=jax.ShapeDtypeStruct(q.shape, BF16),
        scratch_shapes=[pltpu.VMEM((3, len(DILATIONS), ATTN_GROUP, grp_rows, LANES), F32)],
        compiler_params=pltpu.CompilerParams(dimension_semantics=("arbitrary",) * 3,
                                             vmem_limit_bytes=VMEM_LIMIT),
        name="attn",
    )(_band_bias(), q, k, k, v, v)


def _experts(meta, idx, h2, w_gu, b_gu, w_down, b_down, n_blocks, n_slots):
    d = w_down.shape[-1]

    def expert_block(i, meta):
        return (meta[1 + i], 0, 0)

    any_spec = pl.BlockSpec(memory_space=pl.ANY)
    return pl.pallas_call(
        _expert_kernel,
        grid_spec=pltpu.PrefetchScalarGridSpec(
            num_scalar_prefetch=1,
            grid=(n_blocks + 1,),
            in_specs=[any_spec, any_spec,
                      pl.BlockSpec((1, d, 2 * D_FF), expert_block),
                      pl.BlockSpec((1, 1, 2 * D_FF), expert_block),
                      pl.BlockSpec((1, D_FF, d), expert_block),
                      pl.BlockSpec((1, 1, d), expert_block)],
            out_specs=any_spec,
            scratch_shapes=[pltpu.VMEM((2, MOE_ROWS * SLABS, LANES), F32),
                            pltpu.VMEM((2, MOE_ROWS * SLABS, LANES), F32),
                            pltpu.SMEM((2 * MOE_ROWS,), jnp.int32), pltpu.SMEM((2 * MOE_ROWS,), jnp.int32),
                            pltpu.VMEM((d, 2 * D_FF), BF16), pltpu.VMEM((D_FF, d), BF16),
                            pltpu.SemaphoreType.DMA(()), pltpu.SemaphoreType.DMA(()),
                            pltpu.SemaphoreType.DMA(())]),
        out_shape=jax.ShapeDtypeStruct((n_slots * SLABS, LANES), F32),
        compiler_params=pltpu.CompilerParams(dimension_semantics=("arbitrary",), vmem_limit_bytes=VMEM_LIMIT),
        name="experts",
    )(meta, idx, h2, w_gu, b_gu.reshape(N_EXPERTS, 1, 2 * D_FF), w_down, b_down.reshape(N_EXPERTS, 1, d))


def _routing(top_idx):
    n_tok = top_idx.shape[1]
    n_asg = n_tok * TOP_K
    i32 = jnp.int32
    slot_of = jnp.arange(n_tok, dtype=i32)[None, :] * TOP_K + jnp.arange(TOP_K, dtype=i32)[:, None]
    key = (top_idx.astype(i32) * n_asg + slot_of).reshape(n_asg)
    skey = lax.sort(key)
    asg_sorted = skey % n_asg
    experts = jnp.arange(N_EXPERTS, dtype=i32)
    counts = jnp.sum((top_idx.reshape(1, n_asg) == experts[:, None]).astype(i32), axis=1)
    starts = jnp.cumsum(counts) - counts
    padded = (counts + MOE_ROWS - 1) // MOE_ROWS * MOE_ROWS
    pends = jnp.cumsum(padded)
    pstarts = pends - padded
    n_blocks = n_asg // MOE_ROWS + N_EXPERTS
    n_used = (pends[-1] // MOE_ROWS).astype(i32)
    blk = jnp.arange(n_blocks + 2, dtype=i32)
    blk_start = blk * MOE_ROWS
    clamped = jnp.minimum(blk_start, pends[-1] - MOE_ROWS)
    blk_expert = jnp.sum((pends[None, :] <= clamped[:, None]).astype(i32), axis=1)
    row = jnp.arange(MOE_ROWS, dtype=i32)[None, :]
    off = blk_start[:, None] + row - pstarts[blk_expert][:, None]
    valid = jnp.logical_and(off < counts[blk_expert][:, None], blk[:, None] < n_used)
    asg = asg_sorted[jnp.clip(starts[blk_expert][:, None] + off, 0, n_asg - 1)]
    tok = jnp.where(valid, asg // TOP_K, 0)
    spare = jnp.broadcast_to(n_asg + row, (2, MOE_ROWS))
    slot = jnp.concatenate([spare, jnp.where(valid, asg, n_asg + row)[:-2]], axis=0)
    idx = jnp.concatenate([tok, slot], axis=1).reshape(-1).astype(i32)
    meta = jnp.concatenate([n_used[None], blk_expert[:n_blocks + 1]])
    return idx, meta, n_blocks


def kernel(x, c, positions, w_ada, b_ada, g_mix, w_in, conv_w, conv_b, conv_ln_g, conv_ln_b,
           w_out, g_ffn, w_router, b_router, w_gu, b_gu, w_down, b_down, g_final):
    bsz, seq, d = x.shape
    n_tok = bsz * seq
    depth = w_ada.shape[0]
    assert depth == 1, "the final RMSNorm is fused into the layer's combine kernel"
    n_chunks = seq // ATTN_CHUNK
    grp_rows = ATTN_CHUNK // ATTN_GROUP
    tile_run = PROJ_ROWS // ATTN_GROUP
    tiles_per_chunk = ATTN_CHUNK // PROJ_ROWS
    grouped_pos = positions.reshape(bsz, n_chunks, grp_rows, ATTN_GROUP).swapaxes(2, 3)
    cos_sin = _rotary_cos_sin(grouped_pos)
    rot_expand, rot_offset = _rotary_expansion()
    perm = _group_perm(PROJ_ROWS)
    s_tiles = seq // PROJ_ROWS
    arb2 = pltpu.CompilerParams(dimension_semantics=("arbitrary", "arbitrary"), vmem_limit_bytes=VMEM_LIMIT)
    arb1 = pltpu.CompilerParams(dimension_semantics=("arbitrary",), vmem_limit_bytes=VMEM_LIMIT)

    for l in range(depth):
        n_mod = 6 * d
        mod = pl.pallas_call(
            _ada_kernel,
            grid=(n_mod // d,),
            in_specs=[pl.BlockSpec((bsz, d), lambda j: (0, 0)),
                      pl.BlockSpec((d, d), lambda j: (0, j)),
                      pl.BlockSpec((1, d), lambda j: (0, j))],
            out_specs=pl.BlockSpec((bsz, d), lambda j: (0, j)),
            out_shape=jax.ShapeDtypeStruct((bsz, n_mod), F32),
            name="ada",
        )(c, w_ada[l], b_ada[l].reshape(1, n_mod))
        mod = mod.reshape(bsz, 6, d)

        row_spec = lambda width: pl.BlockSpec((1, PROJ_ROWS, width), lambda b, s: (b, s, 0))
        grp_spec = lambda width: pl.BlockSpec(
            (1, 1, ATTN_GROUP, tile_run, width), lambda b, s: (b, s // tiles_per_chunk, 0, s % tiles_per_chunk, 0))
        grp_shape = lambda width, dtype: jax.ShapeDtypeStruct((bsz, n_chunks, ATTN_GROUP, grp_rows, width), dtype)
        full2 = lambda shape: pl.BlockSpec(shape, lambda b, s: (0, 0))
        mod_spec = pl.BlockSpec((1, 6, d), lambda b, s: (b, 0, 0))
        conv_out, q, k, v = pl.pallas_call(
            _proj_kernel,
            grid=(bsz, s_tiles),
            in_specs=[row_spec(d), mod_spec, full2((1, d)), full2((d, IN_COLS)), full2((PROJ_ROWS, PROJ_ROWS)),
                      full2((CONV_WIDTH, CONV_CH)), full2((1, CONV_CH)), full2((1, CONV_CH)), full2((1, CONV_CH)),
                      grp_spec(ROT_DIM), full2((3 * ROT_DIM, 3 * LANES)), full2((1, 3 * LANES))],
            out_specs=[row_spec(CONV_CH), grp_spec(ATTN_WIDTH), grp_spec(ATTN_WIDTH), grp_spec(ATTN_WIDTH)],
            out_shape=[jax.ShapeDtypeStruct((bsz, seq, CONV_CH), BF16),
                       grp_shape(ATTN_WIDTH, F32), grp_shape(ATTN_WIDTH, F32), grp_shape(ATTN_WIDTH, F32)],
            scratch_shapes=[pltpu.VMEM((PROJ_ROWS + CONV_HALO, CONV_CH), F32)],
            compiler_params=arb2,
            name="proj",
        )(x, mod, g_mix[l].reshape(1, d), w_in[l].astype(BF16), perm, conv_w[l].reshape(CONV_WIDTH, CONV_CH),
          conv_b[l].reshape(1, CONV_CH), conv_ln_g[l].reshape(1, CONV_CH), conv_ln_b[l].reshape(1, CONV_CH),
          cos_sin, rot_expand, rot_offset)

        attn_out = _attention(q, k, v)

        tok_cols = pl.BlockSpec((TOP_K, PROJ_ROWS), lambda b, s: (0, b * s_tiles + s))
        x1, h2, top_idx, gates = pl.pallas_call(
            _mix_kernel,
            grid=(bsz, s_tiles),
            in_specs=[row_spec(CONV_CH), grp_spec(ATTN_WIDTH), row_spec(d), mod_spec, full2((d, d)),
                      full2((PROJ_ROWS, PROJ_ROWS)), full2((1, d)), full2((N_EXPERTS, d)), full2((N_EXPERTS, 1))],
            out_specs=[row_spec(d),
                       pl.BlockSpec((PROJ_ROWS * SLABS, LANES), lambda b, s: (b * s_tiles + s, 0)),
                       tok_cols, tok_cols],
            out_shape=[jax.ShapeDtypeStruct((bsz, seq, d), F32),
                       jax.ShapeDtypeStruct((n_tok * SLABS, LANES), F32),
                       jax.ShapeDtypeStruct((TOP_K, n_tok), jnp.int32),
                       jax.ShapeDtypeStruct((TOP_K, n_tok), F32)],
            compiler_params=arb2,
            name="mix",
        )(conv_out, attn_out, x, mod, w_out[l].astype(BF16), perm.T, g_ffn[l].reshape(1, d),
          w_router[l].T.astype(BF16), b_router[l].reshape(N_EXPERTS, 1))
        gates = gates.T

        idx, meta, n_blocks = _routing(top_idx)
        n_slots = n_tok * TOP_K + MOE_ROWS
        n_tiles = n_tok // TOK_TILE

        y = _experts(meta, idx, h2, w_gu[l], b_gu[l], w_down[l], b_down[l], n_blocks, n_slots)

        out = pl.pallas_call(
            _combine_kernel,
            grid_spec=pltpu.PrefetchScalarGridSpec(
                num_scalar_prefetch=1,
                grid=(n_tiles,),
                in_specs=[pl.BlockSpec((TOK_TILE, d), lambda i, g: (i, 0)),
                          pl.BlockSpec((1, 6, d), lambda i, g: (i // (seq // TOK_TILE), 0, 0)),
                          pl.BlockSpec((1, d), lambda i, g: (0, 0)),
                          pl.BlockSpec((TOK_TILE * TOP_K * SLABS, LANES), lambda i, g: (i, 0))],
                out_specs=pl.BlockSpec((TOK_TILE, d), lambda i, g: (i, 0)),
                scratch_shapes=[pltpu.VMEM((TOK_TILE * SLABS, LANES), F32)]),
            out_shape=jax.ShapeDtypeStruct((n_tok, d), F32),
            compiler_params=arb1,
            name="combine",
        )(gates.reshape(-1), x1.reshape(n_tok, d), mod, g_final.reshape(1, d), y)
        x = out.reshape(bsz, seq, d)
    return x
```

```python
import functools

import jax
import numpy as np
import jax.numpy as jnp
from jax import lax
from jax.experimental import pallas as pl
from jax.experimental.pallas import tpu as pltpu

F32 = jnp.float32
BF16 = jnp.bfloat16

D_MODEL = 1024
N_HEADS = 8
HEAD_DIM = 64
ATTN_WIDTH = N_HEADS * HEAD_DIM
CONV_CH = D_MODEL - ATTN_WIDTH
IN_COLS = 2 * CONV_CH + 3 * ATTN_WIDTH
CONV_WIDTH = 31
ROT_DIM = HEAD_DIM // 4
ROPE_THETA = 500000.0
DILATIONS = (1, 4, 16)
ATTN_BLK = 128
ATTN_GROUP = 16
N_EXPERTS = 32
TOP_K = 4
D_FF = D_MODEL
SWIGLU_LIMIT = 7.0
SWIGLU_ALPHA = 1.702
NORM_EPS = 1e-6

LANES = 128
SUBLANES = 8
SLABS = D_MODEL // LANES
VMEM_LIMIT = 56 * 1024 * 1024

PROJ_ROWS = 512
CONV_HALO = 32
CONV_ROWS = 64
ATTN_CHUNK = 2048
MOE_ROWS = 512
MOE_COLS = 1024
TOK_TILE = 256


def _ada_kernel(c_ref, w_ref, b_ref, o_ref):
    c = c_ref[...]
    c_act = c * jax.nn.sigmoid(c)
    o_ref[...] = jnp.dot(c_act.astype(BF16), w_ref[...].astype(BF16), preferred_element_type=F32) + b_ref[...]


def _rms_modulate(x, gain, shift, scale):
    ms = jnp.mean(x * x, axis=-1, keepdims=True)
    y = x * lax.rsqrt(ms + NORM_EPS) * gain
    return y * (1.0 + scale) + shift


def _proj_kernel(x_ref, mod_ref, gmix_ref, win_ref, perm_ref, cw_ref, cb_ref, lng_ref, lnb_ref,
                 cs_ref, rexp_ref, roff_ref, co_ref, q_ref, k_ref, v_ref, ubuf):
    rows = x_ref.shape[1]
    hb = _rms_modulate(x_ref[0], gmix_ref[...], mod_ref[0, 0:1, :], mod_ref[0, 1:2, :]).astype(BF16)

    ag = jnp.dot(hb, win_ref[:, :2 * CONV_CH], preferred_element_type=F32)
    u = ag[:, :CONV_CH] * jax.nn.sigmoid(ag[:, CONV_CH:])

    @pl.when(pl.program_id(1) == 0)
    def _():
        ubuf[0:CONV_HALO, :] = jnp.zeros((CONV_HALO, CONV_CH), F32)

    ubuf[CONV_HALO:, :] = u
    first_tap = CONV_HALO - (CONV_WIDTH - 1)
    for ch in range(rows // CONV_ROWS):
        row0 = ch * CONV_ROWS
        acc = jnp.broadcast_to(cb_ref[...], (CONV_ROWS, CONV_CH))
        for shift in range(SUBLANES):
            taps = [j for j in range(CONV_WIDTH) if (first_tap + j) % SUBLANES == shift]
            span = CONV_ROWS + (SUBLANES if shift else 0)
            part = None
            for j in taps:
                base = row0 + first_tap + j - shift
                term = cw_ref[j:j + 1, :] * ubuf[base:base + span, :]
                part = term if part is None else part + term
            acc = acc + part[shift:shift + CONV_ROWS]
        mu = jnp.mean(acc, axis=-1, keepdims=True)
        cen = acc - mu
        var = jnp.mean(cen * cen, axis=-1, keepdims=True)
        yn = cen * lax.rsqrt(var + NORM_EPS) * lng_ref[...] + lnb_ref[...]
        co_ref[0, ch * CONV_ROWS:(ch + 1) * CONV_ROWS, :] = (yn * jax.nn.sigmoid(yn)).astype(co_ref.dtype)
    ubuf[0:CONV_HALO, :] = ubuf[rows:rows + CONV_HALO, :]

    hp = jnp.dot(perm_ref[...], hb, preferred_element_type=F32).astype(BF16)
    qkv = jnp.dot(hp, win_ref[:, 2 * CONV_CH:], preferred_element_type=F32)
    run = rows // ATTN_GROUP
    tile = (ATTN_GROUP, run, LANES)
    cs = cs_ref[0, 0].reshape(rows, ROT_DIM)
    hi = cs.astype(BF16)
    rest = cs - hi.astype(F32)
    mid = rest.astype(BF16)
    low = (rest - mid.astype(F32)).astype(BF16)
    tables = jnp.dot(jnp.concatenate([hi, mid, low], axis=1), rexp_ref[...],
                     preferred_element_type=F32) + roff_ref[...]
    rc, rsa, rsb = (tables[:, i * LANES:(i + 1) * LANES] for i in range(3))
    half = ROT_DIM // 2
    for blk in range(ATTN_WIDTH // LANES):
        lo, hi = blk * LANES, (blk + 1) * LANES
        for src0, dst_ref, scale in ((0, q_ref, HEAD_DIM ** -0.5), (ATTN_WIDTH, k_ref, 1.0)):
            t = qkv[:, src0 + lo:src0 + hi]
            rot = t * rc + pltpu.roll(t, LANES - half, 1) * rsa + pltpu.roll(t, half, 1) * rsb
            dst_ref[0, 0, :, :, lo:hi] = (rot * scale).reshape(tile)
        v_ref[0, 0, :, :, lo:hi] = qkv[:, 2 * ATTN_WIDTH + lo:2 * ATTN_WIDTH + hi].reshape(tile)


def _attn_kernel(bias_ref, q_ref, kp_ref, kc_ref, vp_ref, vc_ref, o_ref, st):
    chunk = pl.program_id(2)
    lane = lax.broadcasted_iota(jnp.int32, (ATTN_BLK, LANES), 1)
    head0 = lane < HEAD_DIM
    kcol = lax.broadcasted_iota(jnp.int32, (ATTN_BLK, 2 * ATTN_BLK), 1)
    neg_inf = jnp.float32(-jnp.inf)
    n_blocks = ATTN_CHUNK // ATTN_BLK
    all_lanes = slice(None)

    for pi, d in enumerate(DILATIONS):
        n_runs = ATTN_GROUP // d
        run = ATTN_BLK // n_runs
        n_sub = n_blocks // d
        bias = bias_ref[pi]
        bias_first = jnp.where(jnp.logical_or(kcol >= ATTN_BLK, chunk > 0), bias, neg_inf)
        for idx in range(n_blocks):
            u, res = idx // d, idx % d
            a_cur = u * run
            prev_k, prev_v, a_prev = (kc_ref, vc_ref, a_cur - run) if u > 0 else (kp_ref, vp_ref, (n_sub - 1) * run)

            def rows_of(ref, a0, lanes):
                parts = [ref[0, 0, i * d + res, a0:a0 + run, lanes] for i in range(n_runs)]
                return parts[0] if n_runs == 1 else jnp.concatenate(parts, axis=0)

            qb = rows_of(q_ref, a_cur, all_lanes)
            kb = jnp.concatenate([rows_of(prev_k, a_prev, all_lanes),
                                  rows_of(kc_ref, a_cur, all_lanes)], axis=0).astype(BF16)
            q2 = jnp.concatenate([jnp.where(head0, qb, 0.0), jnp.where(head0, 0.0, qb)], axis=0).astype(BF16)
            s2 = lax.dot_general(q2, kb, (((1,), (1,)), ((), ())), preferred_element_type=F32)
            vb = jnp.concatenate([rows_of(prev_v, a_prev, all_lanes),
                                  rows_of(vc_ref, a_cur, all_lanes)], axis=0).astype(BF16)
            parts = []
            for hsel in range(2):
                s = s2[hsel * ATTN_BLK:(hsel + 1) * ATTN_BLK] + (bias if u > 0 else bias_first)
                m = jnp.max(s, axis=1, keepdims=True)
                p = jnp.exp(s - m)
                parts.append((jnp.dot(p.astype(BF16), vb, preferred_element_type=F32),
                              m, jnp.sum(p, axis=1, keepdims=True)))
            vals = [jnp.where(head0, a, b) for a, b in zip(*parts)]
            for i in range(n_runs):
                for si, val in enumerate(vals):
                    st[si, pi, i * d + res, a_cur:a_cur + run, :] = val[i * run:(i + 1) * run]

    def merge(g, carry):
        maxes = [st[1, pi, g] for pi in range(len(DILATIONS))]
        top = functools.reduce(jnp.maximum, maxes)
        weights = [jnp.exp(mx - top) for mx in maxes]
        num = sum(w * st[0, pi, g] for pi, w in enumerate(weights))
        den = sum(w * st[2, pi, g] for pi, w in enumerate(weights))
        o_ref[0, 0, g] = (num / den).astype(o_ref.dtype)
        return carry

    lax.fori_loop(0, ATTN_GROUP, merge, 0, unroll=2)


def _top_k_gates(logits):
    expert = lax.broadcasted_iota(jnp.int32, logits.shape, 0).astype(F32)
    vals, idxs = [], []
    rest = logits
    for _ in range(TOP_K):
        best = jnp.max(rest, axis=0, keepdims=True)
        where_best = jnp.min(jnp.where(rest == best, expert, float(N_EXPERTS)), axis=0, keepdims=True)
        vals.append(best)
        idxs.append(where_best)
        rest = jnp.where(expert == where_best, -jnp.inf, rest)
    weights = [jnp.exp(v - vals[0]) for v in vals]
    total = functools.reduce(lambda a, b: a + b, weights)
    gates = jnp.concatenate([w / total for w in weights], axis=0)
    return jnp.concatenate(idxs, axis=0).astype(jnp.int32), gates


def _mix_kernel(co_ref, ao_ref, x_ref, mod_ref, wo_ref, unperm_ref, gffn_ref, wr_ref, br_ref,
                x1_ref, h2_ref, idx_ref, gate_ref):
    rows = x_ref.shape[1]
    ao = jnp.dot(unperm_ref[...], ao_ref[0, 0].reshape(rows, ATTN_WIDTH),
                 preferred_element_type=F32).astype(BF16)
    mixed = (jnp.dot(co_ref[0], wo_ref[0:CONV_CH, :], preferred_element_type=F32)
             + jnp.dot(ao, wo_ref[CONV_CH:, :], preferred_element_type=F32))
    x1 = x_ref[0] + mod_ref[0, 2:3, :] * mixed
    x1_ref[0] = x1
    h2 = _rms_modulate(x1, gffn_ref[...], mod_ref[0, 3:4, :], mod_ref[0, 4:5, :])
    for j in range(SLABS):
        h2_ref[pl.ds(j, rows, stride=SLABS), :] = h2[:, j * LANES:(j + 1) * LANES]
    logits = lax.dot_general(wr_ref[...], h2.astype(BF16), (((1,), (1,)), ((), ())),
                             preferred_element_type=F32) + br_ref[...]
    idx_ref[...], gate_ref[...] = _top_k_gates(logits)


def _slab_rows(idx):
    return pl.ds(pl.multiple_of(idx * SLABS, SLABS), SLABS)


def _expert_kernel(meta_ref, idx_hbm, h2_hbm, wgu_ref, bgu_ref, wd_ref, bd_ref, y_hbm,
                   xbuf, ybuf, idx_a, idx_b, wgu_bf, wd_bf, gsem, ssem, isem):
    i = pl.program_id(0)
    n_used = meta_ref[0]
    expert = meta_ref[1 + i]
    prev_expert = meta_ref[jnp.maximum(i, 1)]
    cur = i % 2
    idx_bufs = (idx_a, idx_b)
    idx_row = 2 * MOE_ROWS

    def idx_copy(row, slot):
        start = pl.multiple_of(row * idx_row, idx_row)
        return pltpu.make_async_copy(idx_hbm.at[pl.ds(start, idx_row)], idx_bufs[slot], isem)

    def row_slab(r):
        return pl.ds(r * SLABS, SLABS) if isinstance(r, int) else _slab_rows(r)

    def gather_copy(idx_slot, buf_slot, r):
        return pltpu.make_async_copy(h2_hbm.at[_slab_rows(idx_bufs[idx_slot][r]), :],
                                     xbuf.at[buf_slot, row_slab(r), :], gsem)

    def scatter_copy(idx_slot, buf_slot, r):
        return pltpu.make_async_copy(ybuf.at[buf_slot, row_slab(r), :],
                                     y_hbm.at[_slab_rows(idx_bufs[idx_slot][MOE_ROWS + r]), :], ssem)

    def start_rows(idx_slot, buf_slot, gather, scatter, unrolled):
        def body(r, carry):
            if gather:
                gather_copy(idx_slot, buf_slot, r).start()
            if scatter:
                scatter_copy(idx_slot, buf_slot, r).start()
            return carry
        if unrolled:
            for r in range(MOE_ROWS):
                body(r, 0)
        else:
            lax.fori_loop(0, MOE_ROWS, body, 0, unroll=8)

    def wait_gathers():
        pltpu.make_async_copy(h2_hbm.at[pl.ds(0, MOE_ROWS * SLABS), :], xbuf.at[0], gsem).wait()

    def wait_scatters():
        pltpu.make_async_copy(ybuf.at[0], y_hbm.at[pl.ds(0, MOE_ROWS * SLABS), :], ssem).wait()

    @pl.when(i == 0)
    def _():
        first = idx_copy(0, 1)
        first.start()
        first.wait()
        start_rows(1, 0, True, False, False)
        wait_gathers()
        ybuf[1] = jnp.zeros((MOE_ROWS * SLABS, LANES), F32)
        second = idx_copy(1, 0)
        second.start()
        second.wait()

    @pl.when(jnp.logical_or(i == 0, expert != prev_expert))
    def _():
        wgu_bf[...] = wgu_ref[0].astype(BF16)
        wd_bf[...] = wd_ref[0].astype(BF16)

    for parity in (0, 1):
        @pl.when(jnp.logical_and(i < n_used, cur == parity))
        def _(parity=parity):
            idx_copy(i + 2, 1 - parity).start()
            start_rows(parity, 1 - parity, True, True, True)

        @pl.when(jnp.logical_and(i == n_used, cur == parity))
        def _(parity=parity):
            start_rows(parity, 1 - parity, False, True, False)
            wait_scatters()

    @pl.when(i < n_used)
    def _():
        x = jnp.concatenate([xbuf[cur, pl.ds(j, MOE_ROWS, stride=SLABS), :] for j in range(SLABS)],
                            axis=1).astype(BF16)
        y = jnp.broadcast_to(bd_ref[0], (MOE_ROWS, D_MODEL))
        for cidx in range(D_FF // MOE_COLS):
            lo = cidx * MOE_COLS
            gate = jnp.dot(x, wgu_bf[:, lo:lo + MOE_COLS], preferred_element_type=F32) + bgu_ref[0, :, lo:lo + MOE_COLS]
            up = (jnp.dot(x, wgu_bf[:, D_FF + lo:D_FF + lo + MOE_COLS], preferred_element_type=F32)
                  + bgu_ref[0, :, D_FF + lo:D_FF + lo + MOE_COLS])
            gate = jnp.minimum(gate, SWIGLU_LIMIT)
            up = jnp.clip(up, -SWIGLU_LIMIT, SWIGLU_LIMIT)
            act = (up + 1.0) * (gate * jax.nn.sigmoid(SWIGLU_ALPHA * gate))
            y = y + jnp.dot(act.astype(BF16), wd_bf[lo:lo + MOE_COLS, :], preferred_element_type=F32)
        for j in range(SLABS):
            ybuf[cur, pl.ds(j, MOE_ROWS, stride=SLABS), :] = y[:, j * LANES:(j + 1) * LANES]
        wait_gathers()
        wait_scatters()
        idx_copy(0, 0).wait()


def _combine_kernel(gate_ref, x1_ref, mod_ref, gfin_ref, y_ref, o_ref, slab):
    base = pl.program_id(0) * (TOK_TILE * TOP_K)

    def token(t, carry):
        acc = None
        for k in range(TOP_K):
            a = t * TOP_K + k
            term = gate_ref[base + a] * y_ref[_slab_rows(a), :]
            acc = term if acc is None else acc + term
        slab[_slab_rows(t), :] = acc
        return carry

    lax.fori_loop(0, TOK_TILE, token, 0, unroll=8)
    ffn = jnp.concatenate([slab[pl.ds(j, TOK_TILE, stride=SLABS), :] for j in range(SLABS)], axis=1)
    x2 = x1_ref[...] + mod_ref[0, 5:6, :] * ffn
    ms = jnp.mean(x2 * x2, axis=-1, keepdims=True)
    o_ref[...] = x2 * lax.rsqrt(ms + NORM_EPS) * gfin_ref[...]


def _rotary_cos_sin(positions):
    half = ROT_DIM // 2
    inv_freq = ROPE_THETA ** (-jnp.arange(0, ROT_DIM, 2, dtype=F32) / ROT_DIM)
    ang = inv_freq.reshape((half,) + (1,) * positions.ndim) * positions.astype(F32)[None]
    return jnp.moveaxis(jnp.concatenate([jnp.cos(ang), jnp.sin(ang)], axis=0), 0, -1)


def _rotary_expansion():
    half = ROT_DIM // 2
    expand = np.zeros((ROT_DIM, 3 * LANES), np.float32)
    offset = np.zeros((1, 3 * LANES), np.float32)
    for lane in range(LANES):
        dim = lane % HEAD_DIM
        if dim < ROT_DIM:
            expand[dim % half, lane] = 1.0
            if dim < half:
                expand[half + dim, LANES + lane] = -1.0
            else:
                expand[half + dim - half, 2 * LANES + lane] = 1.0
        else:
            offset[0, lane] = 1.0
    return jnp.asarray(np.tile(expand, (3, 1)), BF16), jnp.asarray(offset)


def _group_perm(rows):
    run = rows // ATTN_GROUP
    dst = np.arange(rows)
    perm = np.zeros((rows, rows), np.float32)
    perm[dst, ATTN_GROUP * (dst % run) + dst // run] = 1.0
    return jnp.asarray(perm, BF16)


def _band_bias():
    pos = np.arange(ATTN_BLK)
    tables = []
    for d in DILATIONS:
        n_runs = ATTN_GROUP // d
        run = ATTN_BLK // n_runs
        strided = (pos % run) * n_runs + pos // run
        dist = strided[:, None] + ATTN_BLK - np.concatenate([strided, strided + ATTN_BLK])[None, :]
        tables.append(np.where((dist >= 0) & (dist <= ATTN_BLK), 0.0, -np.inf))
    return jnp.asarray(np.stack(tables), F32)


def _attention(q, k, v):
    bsz, n_chunks, _, grp_rows, _ = q.shape

    def chunk_spec(width, back):
        return pl.BlockSpec((1, 1, ATTN_GROUP, grp_rows, width),
                            lambda b, p, ch: (b, jnp.maximum(ch - back, 0), 0, 0, p))
    return pl.pallas_call(
        _attn_kernel,
        grid=(bsz, ATTN_WIDTH // LANES, n_chunks),
        in_specs=[pl.BlockSpec((len(DILATIONS), ATTN_BLK, 2 * ATTN_BLK), lambda b, p, ch: (0, 0, 0)),
                  chunk_spec(LANES, 0), chunk_spec(LANES, 1), chunk_spec(LANES, 0),
                  chunk_spec(LANES, 1), chunk_spec(LANES, 0)],
        out_specs=chunk_spec(LANES, 0),
        out_shape=jax.ShapeDtypeStruct(q.shape, BF16),
        scratch_shapes=[pltpu.VMEM((3, len(DILATIONS), ATTN_GROUP, grp_rows, LANES), F32)],
        compiler_params=pltpu.CompilerParams(dimension_semantics=("arbitrary",) * 3,
                                             vmem_limit_bytes=VMEM_LIMIT),
        name="attn",
    )(_band_bias(), q, k, k, v, v)


def _experts(meta, idx, h2, w_gu, b_gu, w_down, b_down, n_blocks, n_slots):
    d = w_down.shape[-1]

    def expert_block(i, meta):
        return (meta[1 + i], 0, 0)

    any_spec = pl.BlockSpec(memory_space=pl.ANY)
    return pl.pallas_call(
        _expert_kernel,
        grid_spec=pltpu.PrefetchScalarGridSpec(
            num_scalar_prefetch=1,
            grid=(n_blocks + 1,),
            in_specs=[any_spec, any_spec,
                      pl.BlockSpec((1, d, 2 * D_FF), expert_block),
                      pl.BlockSpec((1, 1, 2 * D_FF), expert_block),
                      pl.BlockSpec((1, D_FF, d), expert_block),
                      pl.BlockSpec((1, 1, d), expert_block)],
            out_specs=any_spec,
            scratch_shapes=[pltpu.VMEM((2, MOE_ROWS * SLABS, LANES), F32),
                            pltpu.VMEM((2, MOE_ROWS * SLABS, LANES), F32),
                            pltpu.SMEM((2 * MOE_ROWS,), jnp.int32), pltpu.SMEM((2 * MOE_ROWS,), jnp.int32),
                            pltpu.VMEM((d, 2 * D_FF), BF16), pltpu.VMEM((D_FF, d), BF16),
                            pltpu.SemaphoreType.DMA(()), pltpu.SemaphoreType.DMA(()),
                            pltpu.SemaphoreType.DMA(())]),
        out_shape=jax.ShapeDtypeStruct((n_slots * SLABS, LANES), F32),
        compiler_params=pltpu.CompilerParams(dimension_semantics=("arbitrary",), vmem_limit_bytes=VMEM_LIMIT),
        name="experts",
    )(meta, idx, h2, w_gu, b_gu.reshape(N_EXPERTS, 1, 2 * D_FF), w_down, b_down.reshape(N_EXPERTS, 1, d))


def _routing(top_idx):
    n_tok = top_idx.shape[1]
    n_asg = n_tok * TOP_K
    i32 = jnp.int32
    slot_of = jnp.arange(n_tok, dtype=i32)[None, :] * TOP_K + jnp.arange(TOP_K, dtype=i32)[:, None]
    key = (top_idx.astype(i32) * n_asg + slot_of).reshape(n_asg)
    skey = lax.sort(key)
    asg_sorted = skey % n_asg
    experts = jnp.arange(N_EXPERTS, dtype=i32)
    counts = jnp.sum((top_idx.reshape(1, n_asg) == experts[:, None]).astype(i32), axis=1)
    starts = jnp.cumsum(counts) - counts
    padded = (counts + MOE_ROWS - 1) // MOE_ROWS * MOE_ROWS
    pends = jnp.cumsum(padded)
    pstarts = pends - padded
    n_blocks = n_asg // MOE_ROWS + N_EXPERTS
    n_used = (pends[-1] // MOE_ROWS).astype(i32)
    blk = jnp.arange(n_blocks + 2, dtype=i32)
    blk_start = blk * MOE_ROWS
    clamped = jnp.minimum(blk_start, pends[-1] - MOE_ROWS)
    blk_expert = jnp.sum((pends[None, :] <= clamped[:, None]).astype(i32), axis=1)
    row = jnp.arange(MOE_ROWS, dtype=i32)[None, :]
    off = blk_start[:, None] + row - pstarts[blk_expert][:, None]
    valid = jnp.logical_and(off < counts[blk_expert][:, None], blk[:, None] < n_used)
    asg = asg_sorted[jnp.clip(starts[blk_expert][:, None] + off, 0, n_asg - 1)]
    tok = jnp.where(valid, asg // TOP_K, 0)
    spare = jnp.broadcast_to(n_asg + row, (2, MOE_ROWS))
    slot = jnp.concatenate([spare, jnp.where(valid, asg, n_asg + row)[:-2]], axis=0)
    idx = jnp.concatenate([tok, slot], axis=1).reshape(-1).astype(i32)
    meta = jnp.concatenate([n_used[None], blk_expert[:n_blocks + 1]])
    return idx, meta, n_blocks


def kernel(x, c, positions, w_ada, b_ada, g_mix, w_in, conv_w, conv_b, conv_ln_g, conv_ln_b,
           w_out, g_ffn, w_router, b_router, w_gu, b_gu, w_down, b_down, g_final):
    bsz, seq, d = x.shape
    n_tok = bsz * seq
    depth = w_ada.shape[0]
    assert depth == 1, "the final RMSNorm is fused into the layer's combine kernel"
    n_chunks = seq // ATTN_CHUNK
    grp_rows = ATTN_CHUNK // ATTN_GROUP
    tile_run = PROJ_ROWS // ATTN_GROUP
    tiles_per_chunk = ATTN_CHUNK // PROJ_ROWS
    grouped_pos = positions.reshape(bsz, n_chunks, grp_rows, ATTN_GROUP).swapaxes(2, 3)
    cos_sin = _rotary_cos_sin(grouped_pos)
    rot_expand, rot_offset = _rotary_expansion()
    perm = _group_perm(PROJ_ROWS)
    s_tiles = seq // PROJ_ROWS
    arb2 = pltpu.CompilerParams(dimension_semantics=("arbitrary", "arbitrary"), vmem_limit_bytes=VMEM_LIMIT)
    arb1 = pltpu.CompilerParams(dimension_semantics=("arbitrary",), vmem_limit_bytes=VMEM_LIMIT)

    for l in range(depth):
        n_mod = 6 * d
        mod = pl.pallas_call(
            _ada_kernel,
            grid=(n_mod // d,),
            in_specs=[pl.BlockSpec((bsz, d), lambda j: (0, 0)),
                      pl.BlockSpec((d, d), lambda j: (0, j)),
                      pl.BlockSpec((1, d), lambda j: (0, j))],
            out_specs=pl.BlockSpec((bsz, d), lambda j: (0, j)),
            out_shape=jax.ShapeDtypeStruct((bsz, n_mod), F32),
            name="ada",
        )(c, w_ada[l], b_ada[l].reshape(1, n_mod))
        mod = mod.reshape(bsz, 6, d)

        row_spec = lambda width: pl.BlockSpec((1, PROJ_ROWS, width), lambda b, s: (b, s, 0))
        grp_spec = lambda width: pl.BlockSpec(
            (1, 1, ATTN_GROUP, tile_run, width), lambda b, s: (b, s // tiles_per_chunk, 0, s % tiles_per_chunk, 0))
        grp_shape = lambda width, dtype: jax.ShapeDtypeStruct((bsz, n_chunks, ATTN_GROUP, grp_rows, width), dtype)
        full2 = lambda shape: pl.BlockSpec(shape, lambda b, s: (0, 0))
        mod_spec = pl.BlockSpec((1, 6, d), lambda b, s: (b, 0, 0))
        conv_out, q, k, v = pl.pallas_call(
            _proj_kernel,
            grid=(bsz, s_tiles),
            in_specs=[row_spec(d), mod_spec, full2((1, d)), full2((d, IN_COLS)), full2((PROJ_ROWS, PROJ_ROWS)),
                      full2((CONV_WIDTH, CONV_CH)), full2((1, CONV_CH)), full2((1, CONV_CH)), full2((1, CONV_CH)),
                      grp_spec(ROT_DIM), full2((3 * ROT_DIM, 3 * LANES)), full2((1, 3 * LANES))],
            out_specs=[row_spec(CONV_CH), grp_spec(ATTN_WIDTH), grp_spec(ATTN_WIDTH), grp_spec(ATTN_WIDTH)],
            out_shape=[jax.ShapeDtypeStruct((bsz, seq, CONV_CH), BF16),
                       grp_shape(ATTN_WIDTH, F32), grp_shape(ATTN_WIDTH, F32), grp_shape(ATTN_WIDTH, F32)],
            scratch_shapes=[pltpu.VMEM((PROJ_ROWS + CONV_HALO, CONV_CH), F32)],
            compiler_params=arb2,
            name="proj",
        )(x, mod, g_mix[l].reshape(1, d), w_in[l].astype(BF16), perm, conv_w[l].reshape(CONV_WIDTH, CONV_CH),
          conv_b[l].reshape(1, CONV_CH), conv_ln_g[l].reshape(1, CONV_CH), conv_ln_b[l].reshape(1, CONV_CH),
          cos_sin, rot_expand, rot_offset)

        attn_out = _attention(q, k, v)

        tok_cols = pl.BlockSpec((TOP_K, PROJ_ROWS), lambda b, s: (0, b * s_tiles + s))
        x1, h2, top_idx, gates = pl.pallas_call(
            _mix_kernel,
            grid=(bsz, s_tiles),
            in_specs=[row_spec(CONV_CH), grp_spec(ATTN_WIDTH), row_spec(d), mod_spec, full2((d, d)),
                      full2((PROJ_ROWS, PROJ_ROWS)), full2((1, d)), full2((N_EXPERTS, d)), full2((N_EXPERTS, 1))],
            out_specs=[row_spec(d),
                       pl.BlockSpec((PROJ_ROWS * SLABS, LANES), lambda b, s: (b * s_tiles + s, 0)),
                       tok_cols, tok_cols],
            out_shape=[jax.ShapeDtypeStruct((bsz, seq, d), F32),
                       jax.ShapeDtypeStruct((n_tok * SLABS, LANES), F32),
                       jax.ShapeDtypeStruct((TOP_K, n_tok), jnp.int32),
                       jax.ShapeDtypeStruct((TOP_K, n_tok), F32)],
            compiler_params=arb2,
            name="mix",
        )(conv_out, attn_out, x, mod, w_out[l].astype(BF16), perm.T, g_ffn[l].reshape(1, d),
          w_router[l].T.astype(BF16), b_router[l].reshape(N_EXPERTS, 1))
        gates = gates.T

        idx, meta, n_blocks = _routing(top_idx)
        n_slots = n_tok * TOP_K + MOE_ROWS
        n_tiles = n_tok // TOK_TILE

        y = _experts(meta, idx, h2, w_gu[l], b_gu[l], w_down[l], b_down[l], n_blocks, n_slots)

        out = pl.pallas_call(
            _combine_kernel,
            grid_spec=pltpu.PrefetchScalarGridSpec(
                num_scalar_prefetch=1,
                grid=(n_tiles,),
                in_specs=[pl.BlockSpec((TOK_TILE, d), lambda i, g: (i, 0)),
                          pl.BlockSpec((1, 6, d), lambda i, g: (i // (seq // TOK_TILE), 0, 0)),
                          pl.BlockSpec((1, d), lambda i, g: (0, 0)),
                          pl.BlockSpec((TOK_TILE * TOP_K * SLABS, LANES), lambda i, g: (i, 0))],
                out_specs=pl.BlockSpec((TOK_TILE, d), lambda i, g: (i, 0)),
                scratch_shapes=[pltpu.VMEM((TOK_TILE * SLABS, LANES), F32)]),
            out_shape=jax.ShapeDtypeStruct((n_tok, d), F32),
            compiler_params=arb1,
            name="combine",
        )(gates.reshape(-1), x1.reshape(n_tok, d), mod, g_final.reshape(1, d), y)
        x = out.reshape(bsz, seq, d)
    return x
```

```python
import functools

import jax
import numpy as np
import jax.numpy as jnp
from jax import lax
from jax.experimental import pallas as pl
from jax.experimental.pallas import tpu as pltpu

F32 = jnp.float32
BF16 = jnp.bfloat16

D_MODEL = 1024
N_HEADS = 8
HEAD_DIM = 64
ATTN_WIDTH = N_HEADS * HEAD_DIM
CONV_CH = D_MODEL - ATTN_WIDTH
IN_COLS = 2 * CONV_CH + 3 * ATTN_WIDTH
CONV_WIDTH = 31
ROT_DIM = HEAD_DIM // 4
ROPE_THETA = 500000.0
DILATIONS = (1, 4, 16)
ATTN_BLK = 128
ATTN_GROUP = 16
N_EXPERTS = 32
TOP_K = 4
D_FF = D_MODEL
SWIGLU_LIMIT = 7.0
SWIGLU_ALPHA = 1.702
NORM_EPS = 1e-6

LANES = 128
SUBLANES = 8
SLABS = D_MODEL // LANES
VMEM_LIMIT = 56 * 1024 * 1024

PROJ_ROWS = 512
CONV_HALO = 32
CONV_ROWS = 64
ATTN_CHUNK = 2048
MOE_ROWS = 512
MOE_COLS = 1024
TOK_TILE = 256


def _ada_kernel(c_ref, w_ref, b_ref, o_ref):
    c = c_ref[...]
    c_act = c * jax.nn.sigmoid(c)
    o_ref[...] = jnp.dot(c_act.astype(BF16), w_ref[...].astype(BF16), preferred_element_type=F32) + b_ref[...]


def _rms_modulate(x, gain, shift, scale):
    ms = jnp.mean(x * x, axis=-1, keepdims=True)
    y = x * lax.rsqrt(ms + NORM_EPS) * gain
    return y * (1.0 + scale) + shift


def _proj_kernel(x_ref, mod_ref, gmix_ref, win_ref, perm_ref, cw_ref, cb_ref, lng_ref, lnb_ref,
                 cs_ref, rexp_ref, roff_ref, co_ref, q_ref, k_ref, v_ref, ubuf):
    rows = x_ref.shape[1]
    hb = _rms_modulate(x_ref[0], gmix_ref[...], mod_ref[0, 0:1, :], mod_ref[0, 1:2, :]).astype(BF16)

    ag = jnp.dot(hb, win_ref[:, :2 * CONV_CH], preferred_element_type=F32)
    u = ag[:, :CONV_CH] * jax.nn.sigmoid(ag[:, CONV_CH:])

    @pl.when(pl.program_id(1) == 0)
    def _():
        ubuf[0:CONV_HALO, :] = jnp.zeros((CONV_HALO, CONV_CH), F32)

    ubuf[CONV_HALO:, :] = u
    first_tap = CONV_HALO - (CONV_WIDTH - 1)
    for ch in range(rows // CONV_ROWS):
        row0 = ch * CONV_ROWS
        acc = jnp.broadcast_to(cb_ref[...], (CONV_ROWS, CONV_CH))
        for shift in range(SUBLANES):
            taps = [j for j in range(CONV_WIDTH) if (first_tap + j) % SUBLANES == shift]
            span = CONV_ROWS + (SUBLANES if shift else 0)
            part = None
            for j in taps:
                base = row0 + first_tap + j - shift
                term = cw_ref[j:j + 1, :] * ubuf[base:base + span, :]
                part = term if part is None else part + term
            acc = acc + part[shift:shift + CONV_ROWS]
        mu = jnp.mean(acc, axis=-1, keepdims=True)
        cen = acc - mu
        var = jnp.mean(cen * cen, axis=-1, keepdims=True)
        yn = cen * lax.rsqrt(var + NORM_EPS) * lng_ref[...] + lnb_ref[...]
        co_ref[0, ch * CONV_ROWS:(ch + 1) * CONV_ROWS, :] = (yn * jax.nn.sigmoid(yn)).astype(co_ref.dtype)
    ubuf[0:CONV_HALO, :] = ubuf[rows:rows + CONV_HALO, :]

    hp = jnp.dot(perm_ref[...], hb, preferred_element_type=F32).astype(BF16)
    qkv = jnp.dot(hp, win_ref[:, 2 * CONV_CH:], preferred_element_type=F32)
    run = rows // ATTN_GROUP
    tile = (ATTN_GROUP, run, LANES)
    cs = cs_ref[0, 0].reshape(rows, ROT_DIM)
    hi = cs.astype(BF16)
    rest = cs - hi.astype(F32)
    mid = rest.astype(BF16)
    low = (rest - mid.astype(F32)).astype(BF16)
    tables = jnp.dot(jnp.concatenate([hi, mid, low], axis=1), rexp_ref[...],
                     preferred_element_type=F32) + roff_ref[...]
    rc, rsa, rsb = (tables[:, i * LANES:(i + 1) * LANES] for i in range(3))
    half = ROT_DIM // 2
    for blk in range(ATTN_WIDTH // LANES):
        lo, hi = blk * LANES, (blk + 1) * LANES
        for src0, dst_ref, scale in ((0, q_ref, HEAD_DIM ** -0.5), (ATTN_WIDTH, k_ref, 1.0)):
            t = qkv[:, src0 + lo:src0 + hi]
            rot = t * rc + pltpu.roll(t, LANES - half, 1) * rsa + pltpu.roll(t, half, 1) * rsb
            dst_ref[0, 0, :, :, lo:hi] = (rot * scale).reshape(tile)
        v_ref[0, 0, :, :, lo:hi] = qkv[:, 2 * ATTN_WIDTH + lo:2 * ATTN_WIDTH + hi].reshape(tile)


def _attn_kernel(bias_ref, q_ref, kp_ref, kc_ref, vp_ref, vc_ref, o_ref, st):
    chunk = pl.program_id(2)
    lane = lax.broadcasted_iota(jnp.int32, (ATTN_BLK, LANES), 1)
    head0 = lane < HEAD_DIM
    kcol = lax.broadcasted_iota(jnp.int32, (ATTN_BLK, 2 * ATTN_BLK), 1)
    neg_inf = jnp.float32(-jnp.inf)
    n_blocks = ATTN_CHUNK // ATTN_BLK
    all_lanes = slice(None)

    for pi, d in enumerate(DILATIONS):
        n_runs = ATTN_GROUP // d
        run = ATTN_BLK // n_runs
        n_sub = n_blocks // d
        bias = bias_ref[pi]
        bias_first = jnp.where(jnp.logical_or(kcol >= ATTN_BLK, chunk > 0), bias, neg_inf)
        for idx in range(n_blocks):
            u, res = idx // d, idx % d
            a_cur = u * run
            prev_k, prev_v, a_prev = (kc_ref, vc_ref, a_cur - run) if u > 0 else (kp_ref, vp_ref, (n_sub - 1) * run)

            def rows_of(ref, a0, lanes):
                parts = [ref[0, 0, i * d + res, a0:a0 + run, lanes] for i in range(n_runs)]
                return parts[0] if n_runs == 1 else jnp.concatenate(parts, axis=0)

            qb = rows_of(q_ref, a_cur, all_lanes)
            kb = jnp.concatenate([rows_of(prev_k, a_prev, all_lanes),
                                  rows_of(kc_ref, a_cur, all_lanes)], axis=0).astype(BF16)
            q2 = jnp.concatenate([jnp.where(head0, qb, 0.0), jnp.where(head0, 0.0, qb)], axis=0).astype(BF16)
            s2 = lax.dot_general(q2, kb, (((1,), (1,)), ((), ())), preferred_element_type=F32)
            vb = jnp.concatenate([rows_of(prev_v, a_prev, all_lanes),
                                  rows_of(vc_ref, a_cur, all_lanes)], axis=0).astype(BF16)
            parts = []
            for hsel in range(2):
                s = s2[hsel * ATTN_BLK:(hsel + 1) * ATTN_BLK] + (bias if u > 0 else bias_first)
                m = jnp.max(s, axis=1, keepdims=True)
                p = jnp.exp(s - m)
                parts.append((jnp.dot(p.astype(BF16), vb, preferred_element_type=F32),
                              m, jnp.sum(p, axis=1, keepdims=True)))
            vals = [jnp.where(head0, a, b) for a, b in zip(*parts)]
            for i in range(n_runs):
                for si, val in enumerate(vals):
                    st[si, pi, i * d + res, a_cur:a_cur + run, :] = val[i * run:(i + 1) * run]

    def merge(g, carry):
        maxes = [st[1, pi, g] for pi in range(len(DILATIONS))]
        top = functools.reduce(jnp.maximum, maxes)
        weights = [jnp.exp(mx - top) for mx in maxes]
        num = sum(w * st[0, pi, g] for pi, w in enumerate(weights))
        den = sum(w * st[2, pi, g] for pi, w in enumerate(weights))
        o_ref[0, 0, g] = (num / den).astype(o_ref.dtype)
        return carry

    lax.fori_loop(0, ATTN_GROUP, merge, 0, unroll=2)


def _top_k_gates(logits):
    expert = lax.broadcasted_iota(jnp.int32, logits.shape, 0).astype(F32)
    vals, idxs = [], []
    rest = logits
    for _ in range(TOP_K):
        best = jnp.max(rest, axis=0, keepdims=True)
        where_best = jnp.min(jnp.where(rest == best, expert, float(N_EXPERTS)), axis=0, keepdims=True)
        vals.append(best)
        idxs.append(where_best)
        rest = jnp.where(expert == where_best, -jnp.inf, rest)
    weights = [jnp.exp(v - vals[0]) for v in vals]
    total = functools.reduce(lambda a, b: a + b, weights)
    gates = jnp.concatenate([w / total for w in weights], axis=0)
    return jnp.concatenate(idxs, axis=0).astype(jnp.int32), gates


def _mix_kernel(co_ref, ao_ref, x_ref, mod_ref, wo_ref, unperm_ref, gffn_ref, wr_ref, br_ref,
                x1_ref, h2_ref, idx_ref, gate_ref):
    rows = x_ref.shape[1]
    ao = jnp.dot(unperm_ref[...], ao_ref[0, 0].reshape(rows, ATTN_WIDTH),
                 preferred_element_type=F32).astype(BF16)
    mixed = (jnp.dot(co_ref[0], wo_ref[0:CONV_CH, :], preferred_element_type=F32)
             + jnp.dot(ao, wo_ref[CONV_CH:, :], preferred_element_type=F32))
    x1 = x_ref[0] + mod_ref[0, 2:3, :] * mixed
    x1_ref[0] = x1
    h2 = _rms_modulate(x1, gffn_ref[...], mod_ref[0, 3:4, :], mod_ref[0, 4:5, :])
    for j in range(SLABS):
        h2_ref[pl.ds(j, rows, stride=SLABS), :] = h2[:, j * LANES:(j + 1) * LANES]
    logits = lax.dot_general(wr_ref[...], h2.astype(BF16), (((1,), (1,)), ((), ())),
                             preferred_element_type=F32) + br_ref[...]
    idx_ref[...], gate_ref[...] = _top_k_gates(logits)


def _slab_rows(idx):
    return pl.ds(pl.multiple_of(idx * SLABS, SLABS), SLABS)


def _expert_kernel(meta_ref, idx_hbm, h2_hbm, wgu_ref, bgu_ref, wd_ref, bd_ref, y_hbm,
                   xbuf, ybuf, idx_a, idx_b, wgu_bf, wd_bf, gsem, ssem, isem):
    i = pl.program_id(0)
    n_used = meta_ref[0]
    expert = meta_ref[1 + i]
    prev_expert = meta_ref[jnp.maximum(i, 1)]
    cur = i % 2
    idx_bufs = (idx_a, idx_b)
    idx_row = 2 * MOE_ROWS

    def idx_copy(row, slot):
        start = pl.multiple_of(row * idx_row, idx_row)
        return pltpu.make_async_copy(idx_hbm.at[pl.ds(start, idx_row)], idx_bufs[slot], isem)

    def row_slab(r):
        return pl.ds(r * SLABS, SLABS) if isinstance(r, int) else _slab_rows(r)

    def gather_copy(idx_slot, buf_slot, r):
        return pltpu.make_async_copy(h2_hbm.at[_slab_rows(idx_bufs[idx_slot][r]), :],
                                     xbuf.at[buf_slot, row_slab(r), :], gsem)

    def scatter_copy(idx_slot, buf_slot, r):
        return pltpu.make_async_copy(ybuf.at[buf_slot, row_slab(r), :],
                                     y_hbm.at[_slab_rows(idx_bufs[idx_slot][MOE_ROWS + r]), :], ssem)

    def start_rows(idx_slot, buf_slot, gather, scatter, unrolled):
        def body(r, carry):
            if gather:
                gather_copy(idx_slot, buf_slot, r).start()
            if scatter:
                scatter_copy(idx_slot, buf_slot, r).start(priority=1)
            return carry
        if unrolled:
            for r in range(MOE_ROWS):
                body(r, 0)
        else:
            lax.fori_loop(0, MOE_ROWS, body, 0, unroll=8)

    def wait_gathers():
        pltpu.make_async_copy(h2_hbm.at[pl.ds(0, MOE_ROWS * SLABS), :], xbuf.at[0], gsem).wait()

    def wait_scatters():
        pltpu.make_async_copy(ybuf.at[0], y_hbm.at[pl.ds(0, MOE_ROWS * SLABS), :], ssem).wait()

    @pl.when(i == 0)
    def _():
        first = idx_copy(0, 1)
        first.start()
        first.wait()
        start_rows(1, 0, True, False, False)
        wait_gathers()
        ybuf[1] = jnp.zeros((MOE_ROWS * SLABS, LANES), F32)
        second = idx_copy(1, 0)
        second.start()
        second.wait()

    @pl.when(jnp.logical_or(i == 0, expert != prev_expert))
    def _():
        wgu_bf[...] = wgu_ref[0].astype(BF16)
        wd_bf[...] = wd_ref[0].astype(BF16)

    for parity in (0, 1):
        @pl.when(jnp.logical_and(i < n_used, cur == parity))
        def _(parity=parity):
            idx_copy(i + 2, 1 - parity).start()
            start_rows(parity, 1 - parity, True, True, True)

        @pl.when(jnp.logical_and(i == n_used, cur == parity))
        def _(parity=parity):
            start_rows(parity, 1 - parity, False, True, False)
            wait_scatters()

    @pl.when(i < n_used)
    def _():
        x = jnp.concatenate([xbuf[cur, pl.ds(j, MOE_ROWS, stride=SLABS), :] for j in range(SLABS)],
                            axis=1).astype(BF16)
        y = jnp.broadcast_to(bd_ref[0], (MOE_ROWS, D_MODEL))
        for cidx in range(D_FF // MOE_COLS):
            lo = cidx * MOE_COLS
            gate = jnp.dot(x, wgu_bf[:, lo:lo + MOE_COLS], preferred_element_type=F32) + bgu_ref[0, :, lo:lo + MOE_COLS]
            up = (jnp.dot(x, wgu_bf[:, D_FF + lo:D_FF + lo + MOE_COLS], preferred_element_type=F32)
                  + bgu_ref[0, :, D_FF + lo:D_FF + lo + MOE_COLS])
            gate = jnp.minimum(gate, SWIGLU_LIMIT)
            up = jnp.clip(up, -SWIGLU_LIMIT, SWIGLU_LIMIT)
            act = (up + 1.0) * (gate * jax.nn.sigmoid(SWIGLU_ALPHA * gate))
            y = y + jnp.dot(act.astype(BF16), wd_bf[lo:lo + MOE_COLS, :], preferred_element_type=F32)
        for j in range(SLABS):
            ybuf[cur, pl.ds(j, MOE_ROWS, stride=SLABS), :] = y[:, j * LANES:(j + 1) * LANES]
        wait_gathers()
        wait_scatters()
        idx_copy(0, 0).wait()


def _combine_kernel(gate_ref, x1_ref, mod_ref, gfin_ref, y_ref, o_ref, slab):
    base = pl.program_id(0) * (TOK_TILE * TOP_K)

    def token(t, carry):
        acc = None
        for k in range(TOP_K):
            a = t * TOP_K + k
            term = gate_ref[base + a] * y_ref[_slab_rows(a), :]
            acc = term if acc is None else acc + term
        slab[_slab_rows(t), :] = acc
        return carry

    lax.fori_loop(0, TOK_TILE, token, 0, unroll=8)
    ffn = jnp.concatenate([slab[pl.ds(j, TOK_TILE, stride=SLABS), :] for j in range(SLABS)], axis=1)
    x2 = x1_ref[...] + mod_ref[0, 5:6, :] * ffn
    ms = jnp.mean(x2 * x2, axis=-1, keepdims=True)
    o_ref[...] = x2 * lax.rsqrt(ms + NORM_EPS) * gfin_ref[...]


def _rotary_cos_sin(positions):
    half = ROT_DIM // 2
    inv_freq = ROPE_THETA ** (-jnp.arange(0, ROT_DIM, 2, dtype=F32) / ROT_DIM)
    ang = inv_freq.reshape((half,) + (1,) * positions.ndim) * positions.astype(F32)[None]
    return jnp.moveaxis(jnp.concatenate([jnp.cos(ang), jnp.sin(ang)], axis=0), 0, -1)


def _rotary_expansion():
    half = ROT_DIM // 2
    expand = np.zeros((ROT_DIM, 3 * LANES), np.float32)
    offset = np.zeros((1, 3 * LANES), np.float32)
    for lane in range(LANES):
        dim = lane % HEAD_DIM
        if dim < ROT_DIM:
            expand[dim % half, lane] = 1.0
            if dim < half:
                expand[half + dim, LANES + lane] = -1.0
            else:
                expand[half + dim - half, 2 * LANES + lane] = 1.0
        else:
            offset[0, lane] = 1.0
    return jnp.asarray(np.tile(expand, (3, 1)), BF16), jnp.asarray(offset)


def _group_perm(rows):
    run = rows // ATTN_GROUP
    dst = np.arange(rows)
    perm = np.zeros((rows, rows), np.float32)
    perm[dst, ATTN_GROUP * (dst % run) + dst // run] = 1.0
    return jnp.asarray(perm, BF16)


def _band_bias():
    pos = np.arange(ATTN_BLK)
    tables = []
    for d in DILATIONS:
        n_runs = ATTN_GROUP // d
        run = ATTN_BLK // n_runs
        strided = (pos % run) * n_runs + pos // run
        dist = strided[:, None] + ATTN_BLK - np.concatenate([strided, strided + ATTN_BLK])[None, :]
        tables.append(np.where((dist >= 0) & (dist <= ATTN_BLK), 0.0, -np.inf))
    return jnp.asarray(np.stack(tables), F32)


def _attention(q, k, v):
    bsz, n_chunks, _, grp_rows, _ = q.shape

    def chunk_spec(width, back):
        return pl.BlockSpec((1, 1, ATTN_GROUP, grp_rows, width),
                            lambda b, p, ch: (b, jnp.maximum(ch - back, 0), 0, 0, p))
    return pl.pallas_call(
        _attn_kernel,
        grid=(bsz, ATTN_WIDTH // LANES, n_chunks),
        in_specs=[pl.BlockSpec((len(DILATIONS), ATTN_BLK, 2 * ATTN_BLK), lambda b, p, ch: (0, 0, 0)),
                  chunk_spec(LANES, 0), chunk_spec(LANES, 1), chunk_spec(LANES, 0),
                  chunk_spec(LANES, 1), chunk_spec(LANES, 0)],
        out_specs=chunk_spec(LANES, 0),
        out_shape=jax.ShapeDtypeStruct(q.shape, BF16),
        scratch_shapes=[pltpu.VMEM((3, len(DILATIONS), ATTN_GROUP, grp_rows, LANES), F32)],
        compiler_params=pltpu.CompilerParams(dimension_semantics=("arbitrary",) * 3,
                                             vmem_limit_bytes=VMEM_LIMIT),
        name="attn",
    )(_band_bias(), q, k, k, v, v)


def _experts(meta, idx, h2, w_gu, b_gu, w_down, b_down, n_blocks, n_slots):
    d = w_down.shape[-1]

    def expert_block(i, meta):
        return (meta[1 + i], 0, 0)

    any_spec = pl.BlockSpec(memory_space=pl.ANY)
    return pl.pallas_call(
        _expert_kernel,
        grid_spec=pltpu.PrefetchScalarGridSpec(
            num_scalar_prefetch=1,
            grid=(n_blocks + 1,),
            in_specs=[any_spec, any_spec,
                      pl.BlockSpec((1, d, 2 * D_FF), expert_block),
                      pl.BlockSpec((1, 1, 2 * D_FF), expert_block),
                      pl.BlockSpec((1, D_FF, d), expert_block),
                      pl.BlockSpec((1, 1, d), expert_block)],
            out_specs=any_spec,
            scratch_shapes=[pltpu.VMEM((2, MOE_ROWS * SLABS, LANES), F32),
                            pltpu.VMEM((2, MOE_ROWS * SLABS, LANES), F32),
                            pltpu.SMEM((2 * MOE_ROWS,), jnp.int32), pltpu.SMEM((2 * MOE_ROWS,), jnp.int32),
                            pltpu.VMEM((d, 2 * D_FF), BF16), pltpu.VMEM((D_FF, d), BF16),
                            pltpu.SemaphoreType.DMA(()), pltpu.SemaphoreType.DMA(()),
                            pltpu.SemaphoreType.DMA(())]),
        out_shape=jax.ShapeDtypeStruct((n_slots * SLABS, LANES), F32),
        compiler_params=pltpu.CompilerParams(dimension_semantics=("arbitrary",), vmem_limit_bytes=VMEM_LIMIT),
        name="experts",
    )(meta, idx, h2, w_gu, b_gu.reshape(N_EXPERTS, 1, 2 * D_FF), w_down, b_down.reshape(N_EXPERTS, 1, d))


def _routing(top_idx):
    n_tok = top_idx.shape[1]
    n_asg = n_tok * TOP_K
    i32 = jnp.int32
    slot_of = jnp.arange(n_tok, dtype=i32)[None, :] * TOP_K + jnp.arange(TOP_K, dtype=i32)[:, None]
    key = (top_idx.astype(i32) * n_asg + slot_of).reshape(n_asg)
    skey = lax.sort(key)
    asg_sorted = skey % n_asg
    experts = jnp.arange(N_EXPERTS, dtype=i32)
    counts = jnp.sum((top_idx.reshape(1, n_asg) == experts[:, None]).astype(i32), axis=1)
    starts = jnp.cumsum(counts) - counts
    padded = (counts + MOE_ROWS - 1) // MOE_ROWS * MOE_ROWS
    pends = jnp.cumsum(padded)
    pstarts = pends - padded
    n_blocks = n_asg // MOE_ROWS + N_EXPERTS
    n_used = (pends[-1] // MOE_ROWS).astype(i32)
    blk = jnp.arange(n_blocks + 2, dtype=i32)
    blk_start = blk * MOE_ROWS
    clamped = jnp.minimum(blk_start, pends[-1] - MOE_ROWS)
    blk_expert = jnp.sum((pends[None, :] <= clamped[:, None]).astype(i32), axis=1)
    row = jnp.arange(MOE_ROWS, dtype=i32)[None, :]
    off = blk_start[:, None] + row - pstarts[blk_expert][:, None]
    valid = jnp.logical_and(off < counts[blk_expert][:, None], blk[:, None] < n_used)
    asg = asg_sorted[jnp.clip(starts[blk_expert][:, None] + off, 0, n_asg - 1)]
    tok = jnp.where(valid, asg // TOP_K, 0)
    spare = jnp.broadcast_to(n_asg + row, (2, MOE_ROWS))
    slot = jnp.concatenate([spare, jnp.where(valid, asg, n_asg + row)[:-2]], axis=0)
    idx = jnp.concatenate([tok, slot], axis=1).reshape(-1).astype(i32)
    meta = jnp.concatenate([n_used[None], blk_expert[:n_blocks + 1]])
    return idx, meta, n_blocks


def kernel(x, c, positions, w_ada, b_ada, g_mix, w_in, conv_w, conv_b, conv_ln_g, conv_ln_b,
           w_out, g_ffn, w_router, b_router, w_gu, b_gu, w_down, b_down, g_final):
    bsz, seq, d = x.shape
    n_tok = bsz * seq
    depth = w_ada.shape[0]
    assert depth == 1, "the final RMSNorm is fused into the layer's combine kernel"
    n_chunks = seq // ATTN_CHUNK
    grp_rows = ATTN_CHUNK // ATTN_GROUP
    tile_run = PROJ_ROWS // ATTN_GROUP
    tiles_per_chunk = ATTN_CHUNK // PROJ_ROWS
    grouped_pos = positions.reshape(bsz, n_chunks, grp_rows, ATTN_GROUP).swapaxes(2, 3)
    cos_sin = _rotary_cos_sin(grouped_pos)
    rot_expand, rot_offset = _rotary_expansion()
    perm = _group_perm(PROJ_ROWS)
    s_tiles = seq // PROJ_ROWS
    arb2 = pltpu.CompilerParams(dimension_semantics=("arbitrary", "arbitrary"), vmem_limit_bytes=VMEM_LIMIT)
    arb1 = pltpu.CompilerParams(dimension_semantics=("arbitrary",), vmem_limit_bytes=VMEM_LIMIT)

    for l in range(depth):
        n_mod = 6 * d
        mod = pl.pallas_call(
            _ada_kernel,
            grid=(n_mod // d,),
            in_specs=[pl.BlockSpec((bsz, d), lambda j: (0, 0)),
                      pl.BlockSpec((d, d), lambda j: (0, j)),
                      pl.BlockSpec((1, d), lambda j: (0, j))],
            out_specs=pl.BlockSpec((bsz, d), lambda j: (0, j)),
            out_shape=jax.ShapeDtypeStruct((bsz, n_mod), F32),
            name="ada",
        )(c, w_ada[l], b_ada[l].reshape(1, n_mod))
        mod = mod.reshape(bsz, 6, d)

        row_spec = lambda width: pl.BlockSpec((1, PROJ_ROWS, width), lambda b, s: (b, s, 0))
        grp_spec = lambda width: pl.BlockSpec(
            (1, 1, ATTN_GROUP, tile_run, width), lambda b, s: (b, s // tiles_per_chunk, 0, s % tiles_per_chunk, 0))
        grp_shape = lambda width, dtype: jax.ShapeDtypeStruct((bsz, n_chunks, ATTN_GROUP, grp_rows, width), dtype)
        full2 = lambda shape: pl.BlockSpec(shape, lambda b, s: (0, 0))
        mod_spec = pl.BlockSpec((1, 6, d), lambda b, s: (b, 0, 0))
        conv_out, q, k, v = pl.pallas_call(
            _proj_kernel,
            grid=(bsz, s_tiles),
            in_specs=[row_spec(d), mod_spec, full2((1, d)), full2((d, IN_COLS)), full2((PROJ_ROWS, PROJ_ROWS)),
                      full2((CONV_WIDTH, CONV_CH)), full2((1, CONV_CH)), full2((1, CONV_CH)), full2((1, CONV_CH)),
                      grp_spec(ROT_DIM), full2((3 * ROT_DIM, 3 * LANES)), full2((1, 3 * LANES))],
            out_specs=[row_spec(CONV_CH), grp_spec(ATTN_WIDTH), grp_spec(ATTN_WIDTH), grp_spec(ATTN_WIDTH)],
            out_shape=[jax.ShapeDtypeStruct((bsz, seq, CONV_CH), BF16),
                       grp_shape(ATTN_WIDTH, F32), grp_shape(ATTN_WIDTH, F32), grp_shape(ATTN_WIDTH, F32)],
            scratch_shapes=[pltpu.VMEM((PROJ_ROWS + CONV_HALO, CONV_CH), F32)],
            compiler_params=arb2,
            name="proj",
        )(x, mod, g_mix[l].reshape(1, d), w_in[l].astype(BF16), perm, conv_w[l].reshape(CONV_WIDTH, CONV_CH),
          conv_b[l].reshape(1, CONV_CH), conv_ln_g[l].reshape(1, CONV_CH), conv_ln_b[l].reshape(1, CONV_CH),
          cos_sin, rot_expand, rot_offset)

        attn_out = _attention(q, k, v)

        tok_cols = pl.BlockSpec((TOP_K, PROJ_ROWS), lambda b, s: (0, b * s_tiles + s))
        x1, h2, top_idx, gates = pl.pallas_call(
            _mix_kernel,
            grid=(bsz, s_tiles),
            in_specs=[row_spec(CONV_CH), grp_spec(ATTN_WIDTH), row_spec(d), mod_spec, full2((d, d)),
                      full2((PROJ_ROWS, PROJ_ROWS)), full2((1, d)), full2((N_EXPERTS, d)), full2((N_EXPERTS, 1))],
            out_specs=[row_spec(d),
                       pl.BlockSpec((PROJ_ROWS * SLABS, LANES), lambda b, s: (b * s_tiles + s, 0)),
                       tok_cols, tok_cols],
            out_shape=[jax.ShapeDtypeStruct((bsz, seq, d), F32),
                       jax.ShapeDtypeStruct((n_tok * SLABS, LANES), F32),
                       jax.ShapeDtypeStruct((TOP_K, n_tok), jnp.int32),
                       jax.ShapeDtypeStruct((TOP_K, n_tok), F32)],
            compiler_params=arb2,
            name="mix",
        )(conv_out, attn_out, x, mod, w_out[l].astype(BF16), perm.T, g_ffn[l].reshape(1, d),
          w_router[l].T.astype(BF16), b_router[l].reshape(N_EXPERTS, 1))
        gates = gates.T

        idx, meta, n_blocks = _routing(top_idx)
        n_slots = n_tok * TOP_K + MOE_ROWS
        n_tiles = n_tok // TOK_TILE

        y = _experts(meta, idx, h2, w_gu[l], b_gu[l], w_down[l], b_down[l], n_blocks, n_slots)

        out = pl.pallas_call(
            _combine_kernel,
            grid_spec=pltpu.PrefetchScalarGridSpec(
                num_scalar_prefetch=1,
                grid=(n_tiles,),
                in_specs=[pl.BlockSpec((TOK_TILE, d), lambda i, g: (i, 0)),
                          pl.BlockSpec((1, 6, d), lambda i, g: (i // (seq // TOK_TILE), 0, 0)),
                          pl.BlockSpec((1, d), lambda i, g: (0, 0)),
                          pl.BlockSpec((TOK_TILE * TOP_K * SLABS, LANES), lambda i, g: (i, 0))],
                out_specs=pl.BlockSpec((TOK_TILE, d), lambda i, g: (i, 0)),
                scratch_shapes=[pltpu.VMEM((TOK_TILE * SLABS, LANES), F32)]),
            out_shape=jax.ShapeDtypeStruct((n_tok, d), F32),
            compiler_params=arb1,
            name="combine",
        )(gates.reshape(-1), x1.reshape(n_tok, d), mod, g_final.reshape(1, d), y)
        x = out.reshape(bsz, seq, d)
    return x
```
